```python
import jax
import jax.numpy as jnp
from jax import lax
import numpy as np

D_MODEL = 1024
BATCH = 16
SEQ = 4096
DEPTH = 2

GRID_W = 64
CTX_LEN = 256
HEAD_DIM = 64
ATTN_SCALE = HEAD_DIM ** -0.5

NA_HEADS = 8
NA_WIDTH = NA_HEADS * HEAD_DIM
WIN_ROWS = 8
WIN_COLS = 16
QCOL_BLK = 16
KCOL_BAND = QCOL_BLK + WIN_COLS

POOL_WINDOWS = (2, 4, 8, 16)
POOL_GROUPS = 4
POOL_GW = 128
POOL_WIDTH = POOL_GROUPS * POOL_GW

SWA_HEADS = 8
SWA_KV_HEADS = 2
SWA_GROUP = SWA_HEADS // SWA_KV_HEADS
SWA_WIDTH = SWA_HEADS * HEAD_DIM
SWA_KV_WIDTH = SWA_KV_HEADS * HEAD_DIM
SWA_WINDOW = 128
SWA_BLK = 128
ROPE_THETA = 10000.0
ROPE_AXIS_DIM = HEAD_DIM // 2

N_BRANCHES = 3
BRANCH_W = 512
IN_DIM = 3 * NA_WIDTH + POOL_WIDTH + SWA_WIDTH + 2 * SWA_KV_WIDTH + N_BRANCHES * D_MODEL

N_GROUPS = 4
EXP_PER_GROUP = 8
N_EXPERTS = N_GROUPS * EXP_PER_GROUP
TOP_K = 2
D_EXPERT = 512
MOE_BLK = 128

N_MOD = 6
NORM_EPS = 1e-6
NEG_INF = -1e30

kernel_name = 'hybrid_natten_pool_swa_hmoe_dit'


def _in_slices():
    widths = (('a_q', NA_WIDTH), ('a_k', NA_WIDTH), ('a_v', NA_WIDTH), ('b_u', POOL_WIDTH),
              ('c_q', SWA_WIDTH), ('c_k', SWA_KV_WIDTH), ('c_v', SWA_KV_WIDTH),
              ('gates', N_BRANCHES * D_MODEL))
    out, off = {}, 0
    for name, w in widths:
        out[name] = slice(off, off + w)
        off += w
    return out


def rms_norm(x, g):
    xf = x.astype(jnp.float32)
    y = xf * lax.rsqrt(jnp.mean(xf * xf, axis=-1, keepdims=True) + NORM_EPS)
    return (y * g).astype(x.dtype)


def adaln_modulation(cvec, w_mod, b_mod):
    m = jax.nn.silu(cvec) @ w_mod + b_mod
    return jnp.split(m, N_MOD, axis=-1)


def modulate(h, shift, scale):
    return h * (1 + scale) + shift


def split_heads(t):
    return t.reshape(t.shape[:-1] + (t.shape[-1] // HEAD_DIM, HEAD_DIM))


def axial_rope_angles(n_tokens):
    t = jnp.arange(n_tokens, dtype=jnp.int32)
    row = (t // GRID_W).astype(jnp.float32)
    col = (t % GRID_W).astype(jnp.float32)
    inv = ROPE_THETA ** (-jnp.arange(0, ROPE_AXIS_DIM, 2, dtype=jnp.float32) / ROPE_AXIS_DIM)
    return row[:, None] * inv[None, :], col[:, None] * inv[None, :]


def _rotate(x, ang):
    cos = jnp.cos(ang)[None, :, None, :]
    sin = jnp.sin(ang)[None, :, None, :]
    x1, x2 = jnp.split(x, 2, axis=-1)
    return jnp.concatenate([x1 * cos - x2 * sin, x2 * cos + x1 * sin], axis=-1)


def axial_rotary(x, ang_r, ang_c):
    xr, xc = jnp.split(x, 2, axis=-1)
    return jnp.concatenate([_rotate(xr, ang_r), _rotate(xc, ang_c)], axis=-1).astype(x.dtype)


def context_attention(q, k, v, sink):
    n_keys = k.shape[1]
    s = jnp.einsum('bqhgd,bkhd->bhgqk', q, k, preferred_element_type=jnp.float32) * ATTN_SCALE
    if sink is not None:
        sink_col = jnp.broadcast_to(sink.astype(jnp.float32)[None, :, :, None, None], s.shape[:-1] + (1,))
        s = jnp.concatenate([s, sink_col], axis=-1)
    p = jax.nn.softmax(s, axis=-1)[..., :n_keys].astype(v.dtype)
    return jnp.einsum('bhgqk,bkhd->bqhgd', p, v)


def neighborhood_attention(q, k, v, kz, vz, rpb):
    B, S, H, dh = q.shape
    rows = S // GRID_W
    kh = min(WIN_ROWS, rows)
    ncb = GRID_W // QCOL_BLK
    qg = q.reshape(B, rows, ncb, QCOL_BLK, H, dh)
    kg = k.reshape(B, rows, GRID_W, H, dh)
    vg = v.reshape(B, rows, GRID_W, H, dh)
    qcol = np.arange(GRID_W).reshape(ncb, QCOL_BLK)
    band0 = np.clip(np.arange(ncb) * QCOL_BLK - WIN_COLS // 2, 0, GRID_W - KCOL_BAND)
    kcol = band0[:, None] + np.arange(KCOL_BAND)[None, :]
    wstart = np.clip(qcol - WIN_COLS // 2, 0, GRID_W - WIN_COLS)
    col_ok = (kcol[:, None, :] >= wstart[:, :, None]) & (kcol[:, None, :] < wstart[:, :, None] + WIN_COLS)
    dcol_idx = np.clip(kcol[:, None, :] - qcol[:, :, None] + WIN_COLS - 1, 0, 2 * WIN_COLS - 2)
    n_win = kh * KCOL_BAND

    def row_block(r):
        r0 = jnp.clip(r - kh // 2, 0, rows - kh)
        kb = lax.dynamic_slice_in_dim(kg, r0, kh, axis=1)[:, :, kcol]
        vb = lax.dynamic_slice_in_dim(vg, r0, kh, axis=1)[:, :, kcol]
        qb = lax.dynamic_index_in_dim(qg, r, axis=1, keepdims=False)
        s_win = jnp.einsum('bjqhd,brjkhd->bhjqrk', qb, kb, preferred_element_type=jnp.float32) * ATTN_SCALE
        drow = r0 + jnp.arange(kh) - r + WIN_ROWS - 1
        bias = rpb[:, drow[None, None, :, None], dcol_idx[:, :, None, :]]
        s_win = jnp.where(col_ok[:, :, None, :], s_win + bias.astype(jnp.float32), NEG_INF)
        s_win = s_win.reshape(B, H, ncb, QCOL_BLK, n_win)
        s_ctx = jnp.einsum('bjqhd,bchd->bhjqc', qb, kz, preferred_element_type=jnp.float32) * ATTN_SCALE
        p = jax.nn.softmax(jnp.concatenate([s_win, s_ctx], axis=-1), axis=-1).astype(v.dtype)
        p_win = p[..., :n_win].reshape(B, H, ncb, QCOL_BLK, kh, KCOL_BAND)
        o = (jnp.einsum('bhjqrk,brjkhd->bjqhd', p_win, vb)
             + jnp.einsum('bhjqc,bchd->bjqhd', p[..., n_win:], vz))
        return o.reshape(B, GRID_W, H, dh)

    out = lax.map(row_block, jnp.arange(rows, dtype=jnp.int32))
    return out.transpose(1, 0, 2, 3, 4).reshape(B, S, H * dh)


def multiscale_pool(u, w_pool, pool_scale):
    B, L, _ = u.shape
    uf = u.astype(jnp.float32)
    cs = jnp.concatenate([jnp.zeros((B, 1, POOL_WIDTH), jnp.float32), jnp.cumsum(uf, axis=1)], axis=1)
    t = np.arange(L)
    diffs = []
    for g, w in enumerate(POOL_WINDOWS):
        lo = np.clip(t - w // 2, 0, L)
        hi = np.clip(t + w // 2, 0, L)
        csg = cs[..., g * POOL_GW:(g + 1) * POOL_GW]
        mean = (csg[:, hi] - csg[:, lo]) / (hi - lo).astype(np.float32)[None, :, None]
        diffs.append(mean - uf[..., g * POOL_GW:(g + 1) * POOL_GW])
    d = jnp.stack(diffs, axis=2).astype(u.dtype)
    y = jnp.einsum('blgc,gce->blge', d, w_pool).reshape(B, L, POOL_WIDTH)
    return y * pool_scale


def windowed_gqa_sink(q, k, v, kz, vz, sink):
    B, S, HQ, dh = q.shape
    nb = S // SWA_BLK
    qg = q.reshape(B, S, SWA_KV_HEADS, SWA_GROUP, dh)
    pad = ((0, 0), (SWA_BLK, SWA_BLK), (0, 0), (0, 0))
    kp = jnp.pad(k, pad)
    vp = jnp.pad(v, pad)
    rel = (np.arange(3 * SWA_BLK) - SWA_BLK)[None, :] - np.arange(SWA_BLK)[:, None]
    near = np.abs(rel) <= SWA_WINDOW
    sink_l = sink.astype(jnp.float32).reshape(SWA_KV_HEADS, SWA_GROUP)[None, :, :, None, None]
    n_win = 3 * SWA_BLK

    def block(i):
        q0 = i * SWA_BLK
        qb = lax.dynamic_slice_in_dim(qg, q0, SWA_BLK, axis=1)
        kb = lax.dynamic_slice_in_dim(kp, q0, n_win, axis=1)
        vb = lax.dynamic_slice_in_dim(vp, q0, n_win, axis=1)
        kpos = q0 - SWA_BLK + jnp.arange(n_win)
        ok = near & ((kpos >= 0) & (kpos < S))[None, :]
        s_win = jnp.einsum('bqhgd,bkhd->bhgqk', qb, kb, preferred_element_type=jnp.float32) * ATTN_SCALE
        s_win = jnp.where(ok, s_win, NEG_INF)
        s_ctx = jnp.einsum('bqhgd,bkhd->bhgqk', qb, kz, preferred_element_type=jnp.float32) * ATTN_SCALE
        s_sink = jnp.broadcast_to(sink_l, s_win.shape[:-1] + (1,))
        p = jax.nn.softmax(jnp.concatenate([s_win, s_ctx, s_sink], axis=-1), axis=-1).astype(v.dtype)
        o = (jnp.einsum('bhgqk,bkhd->bqhgd', p[..., :n_win], vb)
             + jnp.einsum('bhgqk,bkhd->bqhgd', p[..., n_win:n_win + kz.shape[1]], vz))
        return o.reshape(B, SWA_BLK, HQ * dh)

    out = lax.map(block, jnp.arange(nb, dtype=jnp.int32))
    return out.transpose(1, 0, 2, 3).reshape(B, S, HQ * dh)


def merge_branches(ya, yb, yc, gate_logits, w_branch, w_out):
    ga, gb, gc = jnp.split(jax.nn.sigmoid(gate_logits), N_BRANCHES, axis=-1)
    m = ga * (ya @ w_branch[0]) + gb * (yb @ w_branch[1]) + gc * (yc @ w_branch[2])
    return m @ w_out


def expert_dispatch(h, experts, weights, w_gate, w_up, w_down):
    T, D = h.shape
    n_assign = T * TOP_K
    n_slots = -(-(n_assign + N_EXPERTS * (MOE_BLK - 1)) // MOE_BLK) * MOE_BLK
    n_blk = n_slots // MOE_BLK
    flat_e = experts.reshape(n_assign)
    order = jnp.argsort(flat_e)
    sorted_e = flat_e[order]
    counts = jnp.bincount(flat_e, length=N_EXPERTS)
    padded = ((counts + MOE_BLK - 1) // MOE_BLK) * MOE_BLK
    pad_end = jnp.cumsum(padded)
    pad_start = pad_end - padded
    start = jnp.cumsum(counts) - counts
    dest = pad_start[sorted_e] + jnp.arange(n_assign, dtype=jnp.int32) - start[sorted_e]
    token_of_slot = jnp.full((n_slots,), T, jnp.int32).at[dest].set(order // TOP_K)
    h_pad = jnp.concatenate([h, jnp.zeros((1, D), h.dtype)], axis=0)
    xs = h_pad[token_of_slot].reshape(n_blk, MOE_BLK, D)
    block_expert = jnp.minimum(
        jnp.searchsorted(pad_end, jnp.arange(n_blk, dtype=jnp.int32) * MOE_BLK, side='right'),
        N_EXPERTS - 1).astype(jnp.int32)

    def run_block(args):
        xb, e = args
        return (jax.nn.silu(xb @ w_gate[e]) * (xb @ w_up[e])) @ w_down[e]

    ys = lax.map(run_block, (xs, block_expert)).reshape(n_slots, D)
    w_sorted = weights.reshape(n_assign)[order].astype(h.dtype)
    return jnp.zeros((T, D), h.dtype).at[order // TOP_K].add(ys[dest] * w_sorted[:, None])


def hierarchical_moe(h, w_rg, b_rg, w_re, b_re, w_gate, w_up, w_down):
    T = h.shape[0]
    lg = jnp.dot(h, w_rg, preferred_element_type=jnp.float32) + b_rg
    pg = jax.nn.softmax(lg, axis=-1)
    _, g_top = lax.top_k(lg, 1)
    p_grp = jnp.take_along_axis(pg, g_top, axis=-1)
    le = (jnp.dot(h, w_re, preferred_element_type=jnp.float32) + b_re).reshape(T, N_GROUPS, EXP_PER_GROUP)
    le_g = jnp.take_along_axis(le, g_top[:, :, None], axis=1)[:, 0]
    top_p, top_i = lax.top_k(jax.nn.softmax(le_g, axis=-1), TOP_K)
    top_p = top_p / jnp.sum(top_p, axis=-1, keepdims=True)
    weights = p_grp * top_p
    experts = (g_top * EXP_PER_GROUP + top_i).astype(jnp.int32)
    return expert_dispatch(h, experts, weights, w_gate, w_up, w_down)


def setup_inputs(seed: int = 0) -> dict:
    key = jax.random.key(seed)
    ks = jax.random.split(key, 23)
    f32 = jnp.float32

    def nrm(k, shape, std):
        return jax.random.normal(k, shape, f32) * std

    D = D_MODEL
    return {
        'x': nrm(ks[0], (BATCH, SEQ, D), 1.0),
        'c': nrm(ks[1], (BATCH, D), 1.0),
        'ctx': nrm(ks[2], (BATCH, CTX_LEN, D), 1.0),
        'c_ctx': nrm(ks[3], (D,), 1.0),
        'w_mod': nrm(ks[4], (DEPTH, D, N_MOD * D), D ** -0.5),
        'b_mod': nrm(ks[5], (DEPTH, N_MOD * D), 0.02),
        'norm1_g': 1.0 + nrm(ks[6], (DEPTH, D), 0.02),
        'norm2_g': 1.0 + nrm(ks[7], (DEPTH, D), 0.02),
        'w_in': nrm(ks[8], (DEPTH, D, IN_DIM), D ** -0.5),
        'rpb_a': nrm(ks[9], (DEPTH, NA_HEADS, 2 * WIN_ROWS - 1, 2 * WIN_COLS - 1), 0.5),
        'w_pool': nrm(ks[10], (DEPTH, POOL_GROUPS, POOL_GW, POOL_GW), POOL_GW ** -0.5),
        'pool_scale': 1.0 + nrm(ks[11], (DEPTH, POOL_WIDTH), 0.1),
        'sink_c': nrm(ks[12], (DEPTH, SWA_HEADS), 1.0),
        'w_branch': nrm(ks[13], (DEPTH, N_BRANCHES, BRANCH_W, D), BRANCH_W ** -0.5),
        'w_out': nrm(ks[14], (DEPTH, D, D), D ** -0.5),
        'w_router_group': nrm(ks[15], (DEPTH, D, N_GROUPS), D ** -0.5),
        'b_router_group': nrm(ks[16], (DEPTH, N_GROUPS), 0.01),
        'w_router_expert': nrm(ks[17], (DEPTH, D, N_EXPERTS), D ** -0.5),
        'b_router_expert': nrm(ks[18], (DEPTH, N_EXPERTS), 0.01),
        'w_exp_gate': nrm(ks[19], (DEPTH, N_EXPERTS, D, D_EXPERT), D ** -0.5),
        'w_exp_up': nrm(ks[20], (DEPTH, N_EXPERTS, D, D_EXPERT), D ** -0.5),
        'w_exp_down': nrm(ks[21], (DEPTH, N_EXPERTS, D_EXPERT, D), D_EXPERT ** -0.5),
        'final_g': 1.0 + nrm(ks[22], (D,), 0.02),
    }


def reference(x, c, ctx, c_ctx, w_mod, b_mod, norm1_g, norm2_g, w_in, rpb_a, w_pool, pool_scale,
              sink_c, w_branch, w_out, w_router_group, b_router_group, w_router_expert,
              b_router_expert, w_exp_gate, w_exp_up, w_exp_down, final_g):
    B, S, D = x.shape
    Lc = ctx.shape[1]
    sl = _in_slices()
    ang_r, ang_c = axial_rope_angles(S)
    z = ctx
    for l in range(DEPTH):
        last = l == DEPTH - 1
        mx = adaln_modulation(c[:, None, :], w_mod[l], b_mod[l])
        mz = adaln_modulation(c_ctx[None, None, :], w_mod[l], b_mod[l])

        hx = modulate(rms_norm(x, norm1_g[l]), mx[0], mx[1])
        hz = modulate(rms_norm(z, norm1_g[l]), mz[0], mz[1])
        px_all = hx @ w_in[l]
        px = {n: px_all[..., s] for n, s in sl.items()}
        if last:
            pz = {n: hz @ w_in[l][:, sl[n]] for n in ('a_k', 'a_v', 'c_k', 'c_v')}
        else:
            pz_all = hz @ w_in[l]
            pz = {n: pz_all[..., s] for n, s in sl.items()}
        kz_a, vz_a = split_heads(pz['a_k']), split_heads(pz['a_v'])
        kz_c, vz_c = split_heads(pz['c_k']), split_heads(pz['c_v'])

        ya = neighborhood_attention(split_heads(px['a_q']), split_heads(px['a_k']),
                                    split_heads(px['a_v']), kz_a, vz_a, rpb_a[l])
        yb = multiscale_pool(px['b_u'], w_pool[l], pool_scale[l])
        q_c = axial_rotary(split_heads(px['c_q']), ang_r, ang_c)
        k_c = axial_rotary(split_heads(px['c_k']), ang_r, ang_c)
        yc = windowed_gqa_sink(q_c, k_c, split_heads(px['c_v']), kz_c, vz_c, sink_c[l])
        x = x + mx[2] * merge_branches(ya, yb, yc, px['gates'], w_branch[l], w_out[l])

        if not last:
            za = context_attention(split_heads(pz['a_q'])[:, :, :, None, :], kz_a, vz_a, None)
            zb = multiscale_pool(pz['b_u'], w_pool[l], pool_scale[l])
            zq_c = split_heads(pz['c_q']).reshape(B, Lc, SWA_KV_HEADS, SWA_GROUP, HEAD_DIM)
            zc = context_attention(zq_c, kz_c, vz_c, sink_c[l].reshape(SWA_KV_HEADS, SWA_GROUP))
            z = z + mz[2] * merge_branches(za.reshape(B, Lc, NA_WIDTH), zb, zc.reshape(B, Lc, SWA_WIDTH),
                                           pz['gates'], w_branch[l], w_out[l])

        moe_params = (w_router_group[l], b_router_group[l], w_router_expert[l], b_router_expert[l],
                      w_exp_gate[l], w_exp_up[l], w_exp_down[l])
        hx = modulate(rms_norm(x, norm2_g[l]), mx[3], mx[4])
        if last:
            fx = hierarchical_moe(hx.reshape(B * S, D), *moe_params).reshape(B, S, D)
        else:
            hz = modulate(rms_norm(z, norm2_g[l]), mz[3], mz[4])
            tokens = jnp.concatenate([hz.reshape(B * Lc, D), hx.reshape(B * S, D)], axis=0)
            f = hierarchical_moe(tokens, *moe_params)
            z = z + mz[5] * f[:B * Lc].reshape(B, Lc, D)
            fx = f[B * Lc:].reshape(B, S, D)
        x = x + mx[5] * fx
    return rms_norm(x, final_g)
```

```python
import functools

import numpy as np
import jax
import jax.numpy as jnp
from jax import lax
from jax.experimental import pallas as pl
from jax.experimental.pallas import tpu as pltpu

F32 = jnp.float32
BF16 = jnp.bfloat16
I32 = jnp.int32
U32 = jnp.uint32

D_MODEL = 1024
GRID_W = 64
HEAD_DIM = 64
ATTN_SCALE = HEAD_DIM ** -0.5
NA_HEADS = 8
WIN_ROWS = 8
WIN_COLS = 16
POOL_WINDOWS = (2, 4, 8, 16)
POOL_GW = 128
SWA_HEADS = 8
SWA_KV_HEADS = 2
SWA_WINDOW = 128
SWA_BLK = 128
ROPE_THETA = 10000.0
ROPE_AXIS_DIM = HEAD_DIM // 2
N_GROUPS = 4
EXP_PER_GROUP = 8
N_EXPERTS = 32
D_EXPERT = 512
N_MOD = 6
NORM_EPS = 1e-6
NEG_INF = -1e30

LANES_V7X = 128
VMEM_LIMIT_V7X = 56 * 1024 * 1024

COL_AQ, COL_AK, COL_AV, COL_BU = 0, 512, 1024, 1536
COL_CQ, COL_CK, COL_CV, COL_GT = 2048, 2560, 2816, 3072
IN_COLS = 6144

TM_PROJ = 512
TM_MERGE = 256
POOL_CHUNK = 512
PLAN_TILE = 512
MOE_BLK = 256
TM_DISPATCH = 512
TM_COMBINE = 256


def _cparams(sem, vmem=VMEM_LIMIT_V7X):
    return pltpu.CompilerParams(dimension_semantics=sem, vmem_limit_bytes=vmem)


def _dot(a, b):
    return jnp.dot(a, b, preferred_element_type=F32)


def _dot_nt(a, b):
    return lax.dot_general(a, b, (((1,), (1,)), ((), ())), preferred_element_type=F32)


def _modvec_kernel(c_ref, w_ref, b_ref, o_ref):
    c = c_ref[...]
    s = c / (1.0 + jnp.exp(-c))
    o_ref[0] = jnp.dot(s, w_ref[0], preferred_element_type=F32,
                       precision=lax.Precision.HIGHEST) + b_ref[0]


def _modvec(cc, w_mod, b_mod):
    depth, d, n = w_mod.shape
    rows = cc.shape[0]
    tn = 1536
    return pl.pallas_call(
        _modvec_kernel,
        grid=(depth, n // tn),
        in_specs=[
            pl.BlockSpec((rows, d), lambda l, j: (0, 0)),
            pl.BlockSpec((1, d, tn), lambda l, j: (l, 0, j)),
            pl.BlockSpec((1, 1, tn), lambda l, j: (l, 0, j)),
        ],
        out_specs=pl.BlockSpec((1, rows, tn), lambda l, j: (l, 0, j)),
        out_shape=jax.ShapeDtypeStruct((depth, rows, n), F32),
        name="modvec",
        compiler_params=_cparams(("arbitrary", "arbitrary")),
    )(cc, w_mod, b_mod.reshape(depth, 1, n))


def _rope_apply(t, cos, sin, first_half):
    outs = []
    for j in range(t.shape[1] // LANES_V7X):
        tj = t[:, j * LANES_V7X:(j + 1) * LANES_V7X]
        partner = jnp.where(first_half, pltpu.roll(tj, LANES_V7X - 16, 1), pltpu.roll(tj, 16, 1))
        outs.append(tj * cos + partner * sin)
    return outs[0] if len(outs) == 1 else jnp.concatenate(outs, axis=1)


def _inproj_kernel(*refs, rope):
    if rope:
        (x_ref, sh_ref, sc_ref, g_ref, w_ref, cos_ref, sin_ref,
         qkv_ref, u_ref, qkc_ref, vc_ref, gt_ref) = refs
    else:
        (x_ref, sh_ref, sc_ref, g_ref, w_ref,
         qkv_ref, u_ref, qkc_ref, vc_ref, gt_ref) = refs
    x = x_ref[0]
    ms = jnp.mean(x * x, axis=-1, keepdims=True)
    y = x * lax.rsqrt(ms + NORM_EPS) * g_ref[...]
    h = (y * (1.0 + sc_ref[0]) + sh_ref[0]).astype(BF16)

    def proj(a, b):
        return _dot(h, w_ref[:, a:b])

    qkv_ref[0, :, 0:512] = (proj(COL_AQ, COL_AQ + 512) * ATTN_SCALE).astype(BF16)
    qkv_ref[0, :, 512:1024] = proj(COL_AK, COL_AK + 512).astype(BF16)
    qkv_ref[0, :, 1024:1536] = proj(COL_AV, COL_AV + 512).astype(BF16)
    u_ref[0] = proj(COL_BU, COL_BU + 512)
    cq = proj(COL_CQ, COL_CQ + 512)
    ck = proj(COL_CK, COL_CK + 256)
    if rope:
        lane = lax.broadcasted_iota(I32, (x.shape[0], LANES_V7X), 1)
        first_half = (lane % 32) < 16
        cos = cos_ref[...]
        sin = sin_ref[...]
        cq = _rope_apply(cq, cos, sin, first_half)
        ck = _rope_apply(ck, cos, sin, first_half)
    qkc_ref[0, :, 0:512] = (cq * ATTN_SCALE).astype(BF16)
    qkc_ref[0, :, 512:768] = ck.astype(BF16)
    vc_ref[0] = proj(COL_CV, COL_CV + 256).astype(BF16)
    for j in range(6):
        gt_ref[0, :, j * 512:(j + 1) * 512] = proj(COL_GT + j * 512, COL_GT + (j + 1) * 512).astype(BF16)


def _inproj(x, shift, scale, gain, w_perm, cos, sin):
    b, l, d = x.shape
    tm = min(TM_PROJ, l)
    rope = cos is not None
    in_specs = [
        pl.BlockSpec((1, tm, d), lambda bi, i: (bi, i, 0)),
        pl.BlockSpec((1, 1, d), lambda bi, i: (bi, 0, 0)),
        pl.BlockSpec((1, 1, d), lambda bi, i: (bi, 0, 0)),
        pl.BlockSpec((1, d), lambda bi, i: (0, 0)),
        pl.BlockSpec((d, IN_COLS), lambda bi, i: (0, 0), pipeline_mode=pl.Buffered(1)),
    ]
    args = [x, shift, scale, gain.reshape(1, d), w_perm]
    if rope:
        in_specs += [pl.BlockSpec((tm, LANES_V7X), lambda bi, i: (i, 0)),
                     pl.BlockSpec((tm, LANES_V7X), lambda bi, i: (i, 0))]
        args += [cos, sin]
    widths = (1536, 512, 768, 256, 3072)
    dtypes = (BF16, F32, BF16, BF16, BF16)
    return pl.pallas_call(
        functools.partial(_inproj_kernel, rope=rope),
        grid=(b, l // tm),
        in_specs=in_specs,
        out_specs=[pl.BlockSpec((1, tm, w), lambda bi, i: (bi, i, 0)) for w in widths],
        out_shape=[jax.ShapeDtypeStruct((b, l, w), dt) for w, dt in zip(widths, dtypes)],
        name="inproj_rope" if rope else "inproj_ctx",
        compiler_params=_cparams(("parallel", "parallel")),
    )(*args)


def _stack_heads(q2):
    n = q2.shape[0]
    row = lax.broadcasted_iota(I32, (2 * n, LANES_V7X), 0)
    lane = lax.broadcasted_iota(I32, (2 * n, LANES_V7X), 1)
    keep = (row < n) == (lane < HEAD_DIM)
    return jnp.where(keep, jnp.concatenate([q2, q2], axis=0), jnp.zeros((), q2.dtype))


def _unstack_heads(o):
    n = o.shape[0] // 2
    lane = lax.broadcasted_iota(I32, (n, LANES_V7X), 1)
    return jnp.where(lane < HEAD_DIM, o[:n], o[n:])


def _softmax_pv(score_blocks, value_blocks, extra_logit=None):
    m = score_blocks[0].max(axis=1, keepdims=True)
    for s in score_blocks[1:]:
        m = jnp.maximum(m, s.max(axis=1, keepdims=True))
    if extra_logit is not None:
        m = jnp.maximum(m, extra_logit)
    denom = None
    acc = None
    for s, v in zip(score_blocks, value_blocks):
        p = jnp.exp(s - m)
        ps = p.sum(axis=1, keepdims=True)
        denom = ps if denom is None else denom + ps
        pv = _dot(p.astype(BF16), v)
        acc = pv if acc is None else acc + pv
    if extra_logit is not None:
        denom = denom + jnp.exp(extra_logit - m)
    return acc / denom


def _natten_kernel(q_ref, k_ref, v_ref, kz_ref, vz_ref, bias_ref, o_ref, *, rows):
    kz = kz_ref[0]
    vz = vz_ref[0]
    win = WIN_ROWS * GRID_W

    def body(r, carry):
        r0 = jnp.clip(r - WIN_ROWS // 2, 0, rows - WIN_ROWS)
        cls = r0 - r + (WIN_ROWS - 1)
        q2 = q_ref[0, pl.ds(pl.multiple_of(r * GRID_W, GRID_W), GRID_W), :]
        k0 = pl.multiple_of(r0 * GRID_W, GRID_W)
        kw = k_ref[0, pl.ds(k0, win), :]
        vw = v_ref[0, pl.ds(k0, win), :]
        qs = _stack_heads(q2)
        s_w = _dot_nt(qs, kw) + bias_ref[0, cls]
        s_c = _dot_nt(qs, kz)
        o = _softmax_pv([s_w, s_c], [vw, vz])
        o_ref[0, pl.ds(pl.multiple_of(r * GRID_W, GRID_W), GRID_W), :] = _unstack_heads(o).astype(BF16)
        return carry

    lax.fori_loop(0, rows, body, 0)


def _natten(qkv_x, qkv_z, bias):
    b, s, _ = qkv_x.shape
    lc = qkv_z.shape[1]
    rows = s // GRID_W
    assert rows >= WIN_ROWS
    npair = NA_HEADS // 2
    blk = lambda off: pl.BlockSpec((1, s, LANES_V7X), lambda hp, bi, off=off: (bi, 0, off + hp))
    blkz = lambda off: pl.BlockSpec((1, lc, LANES_V7X), lambda hp, bi, off=off: (bi, 0, off + hp))
    return pl.pallas_call(
        functools.partial(_natten_kernel, rows=rows),
        grid=(npair, b),
        in_specs=[blk(0), blk(npair), blk(2 * npair), blkz(npair), blkz(2 * npair),
                  pl.BlockSpec((1, WIN_ROWS, 2 * GRID_W, WIN_ROWS * GRID_W), lambda hp, bi: (hp, 0, 0, 0))],
        out_specs=pl.BlockSpec((1, s, LANES_V7X), lambda hp, bi: (bi, 0, hp)),
        out_shape=jax.ShapeDtypeStruct((b, s, NA_HEADS * HEAD_DIM), BF16),
        name="natten",
        compiler_params=_cparams(("parallel", "parallel")),
    )(qkv_x, qkv_x, qkv_x, qkv_z, qkv_z, bias)


def _natten_bias(rpb):
    h = rpb.shape[0]
    qc = np.arange(GRID_W)
    kc = np.arange(GRID_W)
    wstart = np.clip(qc - WIN_COLS // 2, 0, GRID_W - WIN_COLS)
    col_ok = (kc[None, :] >= wstart[:, None]) & (kc[None, :] < wstart[:, None] + WIN_COLS)
    dcol = np.clip(kc[None, :] - qc[:, None] + WIN_COLS - 1, 0, 2 * WIN_COLS - 2)
    e = jnp.where(col_ok[None, None], rpb[:, :, dcol].astype(F32), NEG_INF)
    idx = np.arange(WIN_ROWS)[:, None] + np.arange(WIN_ROWS)[None, :]
    bc = e[:, idx]
    bc = bc.transpose(0, 1, 3, 2, 4).reshape(h, WIN_ROWS, GRID_W, WIN_ROWS * GRID_W)
    bc = bc.reshape(h // 2, 2, WIN_ROWS, GRID_W, WIN_ROWS * GRID_W).transpose(0, 2, 1, 3, 4)
    return bc.reshape(h // 2, WIN_ROWS, 2 * GRID_W, WIN_ROWS * GRID_W)


def _swa_kernel(sink_ref, q_ref, k_ref, v_ref, kz_ref, vz_ref, mask_ref, o_ref, *, seq):
    g = pl.program_id(0)
    kz = kz_ref[0]
    vz = vz_ref[0]
    nwin = 3 * SWA_BLK
    group = SWA_HEADS // SWA_KV_HEADS
    row = lax.broadcasted_iota(I32, (2 * SWA_BLK, 1), 0)

    def body(i, carry):
        q0 = pl.multiple_of(i * SWA_BLK, SWA_BLK)
        start = pl.multiple_of(jnp.clip(q0 - SWA_BLK, 0, seq - nwin), SWA_BLK)
        var = (q0 - start) // SWA_BLK
        kw = k_ref[0, pl.ds(start, nwin), :]
        vw = v_ref[0, pl.ds(start, nwin), :]
        mb = mask_ref[var]
        mb2 = jnp.concatenate([mb, mb], axis=0)
        for jj in range(group // 2):
            q2 = q_ref[0, pl.ds(q0, SWA_BLK), jj * LANES_V7X:(jj + 1) * LANES_V7X]
            qs = _stack_heads(q2)
            s_w = _dot_nt(qs, kw) + mb2
            s_c = _dot_nt(qs, kz)
            sink = jnp.where(row < SWA_BLK, sink_ref[group * g + 2 * jj], sink_ref[group * g + 2 * jj + 1])
            o = _softmax_pv([s_w, s_c], [vw, vz], extra_logit=sink)
            o_ref[0, pl.ds(q0, SWA_BLK), jj * LANES_V7X:(jj + 1) * LANES_V7X] = _unstack_heads(o).astype(BF16)
        return carry

    lax.fori_loop(0, seq // SWA_BLK, body, 0)


def _swa(qkc_x, vc_x, qkc_z, vc_z, mask_bias, sink):
    b, s, _ = qkc_x.shape
    lc = qkc_z.shape[1]
    assert s % SWA_BLK == 0 and s >= 3 * SWA_BLK
    qw = 2 * LANES_V7X
    return pl.pallas_call(
        functools.partial(_swa_kernel, seq=s),
        grid=(SWA_KV_HEADS, b),
        in_specs=[
            pl.BlockSpec(memory_space=pltpu.SMEM),
            pl.BlockSpec((1, s, qw), lambda g, bi: (bi, 0, g)),
            pl.BlockSpec((1, s, LANES_V7X), lambda g, bi: (bi, 0, 4 + g)),
            pl.BlockSpec((1, s, LANES_V7X), lambda g, bi: (bi, 0, g)),
            pl.BlockSpec((1, lc, LANES_V7X), lambda g, bi: (bi, 0, 4 + g)),
            pl.BlockSpec((1, lc, LANES_V7X), lambda g, bi: (bi, 0, g)),
            pl.BlockSpec((3, SWA_BLK, 3 * SWA_BLK), lambda g, bi: (0, 0, 0)),
        ],
        out_specs=pl.BlockSpec((1, s, qw), lambda g, bi: (bi, 0, g)),
        out_shape=jax.ShapeDtypeStruct((b, s, SWA_HEADS * HEAD_DIM), BF16),
        name="swa",
        compiler_params=_cparams(("parallel", "parallel")),
    )(sink, qkc_x, qkc_x, vc_x, qkc_z, vc_z, mask_bias)


def _swa_mask_bias():
    i = np.arange(SWA_BLK)[:, None]
    j = np.arange(3 * SWA_BLK)[None, :]
    out = np.zeros((3, SWA_BLK, 3 * SWA_BLK), np.float32)
    for v in range(3):
        rel = j - v * SWA_BLK - i
        out[v] = np.where(np.abs(rel) <= SWA_WINDOW, 0.0, NEG_INF)
    return jnp.asarray(out)


def _ctx_attn_kernel(*refs, ngroups, use_sink):
    if use_sink:
        sink_ref, q_ref, k_ref, v_ref, o_ref = refs
    else:
        q_ref, k_ref, v_ref, o_ref = refs
    g = pl.program_id(0)
    k = k_ref[0]
    v = v_ref[0]
    n = q_ref.shape[1]
    row = lax.broadcasted_iota(I32, (2 * n, 1), 0)
    for jj in range(ngroups):
        qs = _stack_heads(q_ref[0, :, jj * LANES_V7X:(jj + 1) * LANES_V7X])
        s = _dot_nt(qs, k)
        sink = None
        if use_sink:
            base = 2 * ngroups * g + 2 * jj
            sink = jnp.where(row < n, sink_ref[base], sink_ref[base + 1])
        o = _softmax_pv([s], [v], extra_logit=sink)
        o_ref[0, :, jj * LANES_V7X:(jj + 1) * LANES_V7X] = _unstack_heads(o).astype(BF16)


def _ctx_attn_a(qkv_z):
    b, lc, _ = qkv_z.shape
    npair = NA_HEADS // 2
    blk = lambda off: pl.BlockSpec((1, lc, LANES_V7X), lambda hp, bi, off=off: (bi, 0, off + hp))
    return pl.pallas_call(
        functools.partial(_ctx_attn_kernel, ngroups=1, use_sink=False),
        grid=(npair, b),
        in_specs=[blk(0), blk(npair), blk(2 * npair)],
        out_specs=pl.BlockSpec((1, lc, LANES_V7X), lambda hp, bi: (bi, 0, hp)),
        out_shape=jax.ShapeDtypeStruct((b, lc, NA_HEADS * HEAD_DIM), BF16),
        name="ctx_attn_a",
        compiler_params=_cparams(("parallel", "parallel")),
    )(qkv_z, qkv_z, qkv_z)


def _ctx_attn_c(qkc_z, vc_z, sink):
    b, lc, _ = qkc_z.shape
    qw = 2 * LANES_V7X
    return pl.pallas_call(
        functools.partial(_ctx_attn_kernel, ngroups=2, use_sink=True),
        grid=(SWA_KV_HEADS, b),
        in_specs=[
            pl.BlockSpec(memory_space=pltpu.SMEM),
            pl.BlockSpec((1, lc, qw), lambda g, bi: (bi, 0, g)),
            pl.BlockSpec((1, lc, LANES_V7X), lambda g, bi: (bi, 0, 4 + g)),
            pl.BlockSpec((1, lc, LANES_V7X), lambda g, bi: (bi, 0, g)),
        ],
        out_specs=pl.BlockSpec((1, lc, qw), lambda g, bi: (bi, 0, g)),
        out_shape=jax.ShapeDtypeStruct((b, lc, SWA_HEADS * HEAD_DIM), BF16),
        name="ctx_attn_c",
        compiler_params=_cparams(("parallel", "parallel")),
    )(sink, qkc_z, qkc_z, vc_z)


PAD_ROWS = 8


def _pool_kernel(u_ref, wp_ref, ps_ref, o_ref, pad_ref, *, length, chunk):
    zeros = jnp.zeros((PAD_ROWS, POOL_GW), F32)
    pad_ref[0:PAD_ROWS, :] = zeros
    pad_ref[length + PAD_ROWS:length + 2 * PAD_ROWS, :] = zeros
    for g, w in enumerate(POOL_WINDOWS):
        half = w // 2
        assert half <= PAD_ROWS
        lanes = slice(g * POOL_GW, (g + 1) * POOL_GW)
        pad_ref[PAD_ROWS:length + PAD_ROWS, :] = u_ref[0, :, lanes]
        for c in range(length // chunk):
            c0 = c * chunk
            acc = pad_ref[c0 + PAD_ROWS - half:c0 + PAD_ROWS - half + chunk, :]
            for j in range(-half + 1, half):
                acc = acc + pad_ref[c0 + PAD_ROWS + j:c0 + PAD_ROWS + j + chunk, :]
            t = c0 + lax.broadcasted_iota(I32, (chunk, 1), 0)
            cnt = (jnp.minimum(t + half, length) - jnp.maximum(t - half, 0)).astype(F32)
            centre = pad_ref[c0 + PAD_ROWS:c0 + PAD_ROWS + chunk, :]
            d = acc / cnt - centre
            y = _dot(d.astype(BF16), wp_ref[g]) * ps_ref[:, lanes]
            o_ref[0, c0:c0 + chunk, lanes] = y.astype(BF16)


def _pool(u, w_pool, pool_scale):
    b, l, width = u.shape
    chunk = min(POOL_CHUNK, l)
    return pl.pallas_call(
        functools.partial(_pool_kernel, length=l, chunk=chunk),
        grid=(b,),
        in_specs=[
            pl.BlockSpec((1, l, width), lambda bi: (bi, 0, 0)),
            pl.BlockSpec(w_pool.shape, lambda bi: (0, 0, 0)),
            pl.BlockSpec((1, width), lambda bi: (0, 0)),
        ],
        out_specs=pl.BlockSpec((1, l, width), lambda bi: (bi, 0, 0)),
        out_shape=jax.ShapeDtypeStruct((b, l, width), BF16),
        scratch_shapes=[pltpu.VMEM((l + 2 * PAD_ROWS, POOL_GW), F32)],
        name="pool",
        compiler_params=_cparams(("parallel",)),
    )(u, w_pool, pool_scale.reshape(1, width))


ROUTER_ROWS = 40


def _sigmoid(x):
    return 1.0 / (1.0 + jnp.exp(-x))


def _merge_kernel(ya_ref, yb_ref, yc_ref, gt_ref, x_ref, wb_ref, wo_ref, g1_ref, n2_ref, sh_ref, sc_ref,
                  wr_ref, br_ref, xo_ref, hp_ref, re_ref, rw_ref):
    d = D_MODEL
    m = None
    for j, y_ref in enumerate((ya_ref, yb_ref, yc_ref)):
        gate = _sigmoid(gt_ref[0, :, j * d:(j + 1) * d].astype(F32))
        term = gate * _dot(y_ref[0], wb_ref[j])
        m = term if m is None else m + term
    out = _dot(m.astype(BF16), wo_ref[...])
    xn = x_ref[0] + g1_ref[0] * out
    xo_ref[0] = xn

    ms = jnp.mean(xn * xn, axis=-1, keepdims=True)
    h = xn * lax.rsqrt(ms + NORM_EPS) * n2_ref[...]
    h = h * (1.0 + sc_ref[0]) + sh_ref[0]

    half = d // 2
    lo = pltpu.bitcast(h[:, :half].astype(BF16).astype(F32), U32)
    hi = pltpu.bitcast(h[:, half:].astype(BF16).astype(F32), U32)
    hp_ref[...] = (lo >> 16) | (hi & jnp.uint32(0xFFFF0000))

    tm = h.shape[0]
    logits = lax.dot_general(wr_ref[...], h, (((1,), (1,)), ((), ())), preferred_element_type=F32,
                             precision=lax.Precision.HIGHEST) + br_ref[:, 0:1]
    le = logits[0:N_EXPERTS]
    lg = logits[N_EXPERTS:N_EXPERTS + N_GROUPS]
    gi = lax.broadcasted_iota(I32, (N_GROUPS, tm), 0).astype(F32)
    lg_max = lg.max(axis=0, keepdims=True)
    g_top = jnp.where(lg == lg_max, gi, float(N_GROUPS)).min(axis=0, keepdims=True)
    p_grp = 1.0 / jnp.exp(lg - lg_max).sum(axis=0, keepdims=True)
    ei_int = lax.broadcasted_iota(I32, (N_EXPERTS, tm), 0)
    ei = ei_int.astype(F32)
    eg = lax.shift_right_logical(ei_int, 3).astype(F32)
    lm = jnp.where(eg == g_top, le, NEG_INF)
    m1 = lm.max(axis=0, keepdims=True)
    i1 = jnp.where(lm == m1, ei, float(N_EXPERTS)).min(axis=0, keepdims=True)
    lm2 = jnp.where(ei == i1, NEG_INF, lm)
    m2 = lm2.max(axis=0, keepdims=True)
    i2 = jnp.where(lm2 == m2, ei, float(N_EXPERTS)).min(axis=0, keepdims=True)
    a2 = jnp.exp(m2 - m1)
    w1 = p_grp / (1.0 + a2)
    re_ref[...] = jnp.concatenate([i1, i2], axis=0).astype(I32)
    rw_ref[...] = jnp.concatenate([w1, w1 * a2], axis=0)


def _merge(ya, yb, yc, gates, x, wb, wo, gate1, norm2_g, shift2, scale2, wr, br):
    b, l, d = x.shape
    tm = min(TM_MERGE, l)
    nt = l // tm
    tok = lambda bi, i: (bi, i, 0)
    per_b = lambda bi, i: (bi, 0, 0)
    const2 = lambda bi, i: (0, 0)
    in_specs = [
        pl.BlockSpec((1, tm, 512), tok), pl.BlockSpec((1, tm, 512), tok), pl.BlockSpec((1, tm, 512), tok),
        pl.BlockSpec((1, tm, 3 * d), tok), pl.BlockSpec((1, tm, d), tok),
        pl.BlockSpec((3, 512, d), lambda bi, i: (0, 0, 0)), pl.BlockSpec((d, d), const2),
        pl.BlockSpec((1, 1, d), per_b), pl.BlockSpec((1, d), const2),
        pl.BlockSpec((1, 1, d), per_b), pl.BlockSpec((1, 1, d), per_b),
        pl.BlockSpec((ROUTER_ROWS, d), const2), pl.BlockSpec((ROUTER_ROWS, LANES_V7X), const2),
    ]
    args = [ya, yb, yc, gates, x, wb, wo, gate1, norm2_g.reshape(1, d), shift2, scale2, wr, br]
    return pl.pallas_call(
        _merge_kernel,
        grid=(b, nt),
        in_specs=in_specs,
        out_specs=[
            pl.BlockSpec((1, tm, d), tok),
            pl.BlockSpec((tm, d // 2), lambda bi, i: (bi * nt + i, 0)),
            pl.BlockSpec((2, tm), lambda bi, i: (0, bi * nt + i)),
            pl.BlockSpec((2, tm), lambda bi, i: (0, bi * nt + i)),
        ],
        out_shape=[
            jax.ShapeDtypeStruct((b, l, d), F32),
            jax.ShapeDtypeStruct((b * l, d // 2), U32),
            jax.ShapeDtypeStruct((2, b * l), I32),
            jax.ShapeDtypeStruct((2, b * l), F32),
        ],
        name="merge_router",
        compiler_params=_cparams(("arbitrary", "arbitrary")),
    )(*args)


def _plan_kernel(re_ref, tri_ref, dest_ref, be_ref, na_ref, cnt_ref, *, ntok, nblk_pad):
    nt = ntok // PLAN_TILE
    ei = lax.broadcasted_iota(I32, (N_EXPERTS, PLAN_TILE), 0)
    cnt_ref[...] = jnp.zeros_like(cnt_ref)

    def rank_tile(i, carry):
        c0 = pl.multiple_of(i * PLAN_TILE, PLAN_TILE)
        e = re_ref[:, pl.ds(c0, PLAN_TILE)]
        hit1 = ei == e[0:1]
        hit2 = ei == e[1:2]
        onehot = jnp.where(hit1 | hit2, 1.0, 0.0)
        before = _dot(onehot.astype(BF16), tri_ref[...]) + cnt_ref[:, 0:1]
        r1 = jnp.where(hit1, before, 0.0).sum(axis=0, keepdims=True)
        r2 = jnp.where(hit2, before, 0.0).sum(axis=0, keepdims=True)
        dest_ref[:, pl.ds(c0, PLAN_TILE)] = jnp.concatenate([r1, r2], axis=0).astype(I32)
        cnt_ref[...] = cnt_ref[...] + onehot.sum(axis=1, keepdims=True)
        return carry

    lax.fori_loop(0, nt, rank_tile, 0)

    cnt = cnt_ref[...]
    padded = jnp.floor((cnt + (MOE_BLK - 1)) * (1.0 / MOE_BLK)) * MOE_BLK
    sub = lax.broadcasted_iota(I32, (N_EXPERTS, LANES_V7X), 0)
    pstart = jnp.zeros_like(padded)
    for e in range(N_EXPERTS - 1):
        pstart = pstart + jnp.where(sub > e, padded[e:e + 1, :], 0.0)
    pend = pstart + padded
    cnt_ref[...] = pstart

    def dest_tile(i, carry):
        c0 = pl.multiple_of(i * PLAN_TILE, PLAN_TILE)
        e = re_ref[:, pl.ds(c0, PLAN_TILE)]
        ps = cnt_ref[:, 0:1]
        s1 = jnp.where(ei == e[0:1], ps, 0.0).sum(axis=0, keepdims=True)
        s2 = jnp.where(ei == e[1:2], ps, 0.0).sum(axis=0, keepdims=True)
        dest_ref[:, pl.ds(c0, PLAN_TILE)] = (dest_ref[:, pl.ds(c0, PLAN_TILE)]
                                              + jnp.concatenate([s1, s2], axis=0).astype(I32))
        return carry

    lax.fori_loop(0, nt, dest_tile, 0)

    first_row = (lax.broadcasted_iota(I32, (N_EXPERTS, nblk_pad), 1) * MOE_BLK).astype(F32)
    owner = jnp.where(pend[:, 0:1] <= first_row, 1.0, 0.0).sum(axis=0, keepdims=True)
    be_ref[...] = jnp.minimum(owner, float(N_EXPERTS - 1)).astype(I32)
    na_ref[...] = (pend[N_EXPERTS - 1:N_EXPERTS, :] * (1.0 / MOE_BLK)).astype(I32)


def _moe_slots(ntok):
    nblk = -(-(2 * ntok + N_EXPERTS * (MOE_BLK - 1)) // MOE_BLK)
    return nblk, -(-nblk // LANES_V7X) * LANES_V7X


def _plan(re, tri):
    ntok = re.shape[1]
    assert ntok % PLAN_TILE == 0
    _, nblk_pad = _moe_slots(ntok)
    return pl.pallas_call(
        functools.partial(_plan_kernel, ntok=ntok, nblk_pad=nblk_pad),
        out_shape=[
            jax.ShapeDtypeStruct((2, ntok), I32),
            jax.ShapeDtypeStruct((1, nblk_pad), I32),
            jax.ShapeDtypeStruct((1, LANES_V7X), I32),
        ],
        scratch_shapes=[pltpu.VMEM((N_EXPERTS, LANES_V7X), F32)],
        name="moe_plan",
        compiler_params=pltpu.CompilerParams(vmem_limit_bytes=VMEM_LIMIT_V7X),
    )(re, tri)


def _dispatch_kernel(dest_ref, hp_ref, xs_in_ref, xs_ref, sem):
    del xs_in_ref
    tm = hp_ref.shape[0]

    def issue(i, carry):
        for k in range(2):
            pltpu.make_async_copy(hp_ref.at[pl.ds(i, 1)], xs_ref.at[pl.ds(dest_ref[k, i], 1)], sem).start()
        return carry

    lax.fori_loop(0, tm, issue, 0, unroll=8)

    def drain(i, carry):
        for k in range(2):
            pltpu.make_async_copy(hp_ref.at[pl.ds(i, 1)], xs_ref.at[pl.ds(dest_ref[k, i], 1)], sem).wait()
        return carry

    lax.fori_loop(0, tm, drain, 0, unroll=8)


def _dispatch(dest, hp, xs, tok_off):
    ntok, width = hp.shape
    tm = min(TM_DISPATCH, ntok)
    assert tok_off % tm == 0 and ntok % tm == 0
    off = tok_off // tm
    return pl.pallas_call(
        _dispatch_kernel,
        grid=(ntok // tm,),
        in_specs=[
            pl.BlockSpec((2, tm), lambda i: (0, off + i), memory_space=pltpu.SMEM),
            pl.BlockSpec((tm, width), lambda i: (i, 0)),
            pl.BlockSpec(memory_space=pl.ANY),
        ],
        out_specs=pl.BlockSpec(memory_space=pl.ANY),
        out_shape=jax.ShapeDtypeStruct(xs.shape, U32),
        scratch_shapes=[pltpu.SemaphoreType.DMA(())],
        input_output_aliases={2: 0},
        name="moe_dispatch",
        compiler_params=_cparams(("arbitrary",)),
    )(dest, hp, xs)


def _experts_kernel(be_ref, na_ref, xs_ref, wg_ref, wu_ref, wd_ref, y_ref):
    del be_ref

    @pl.when(pl.program_id(0) < na_ref[0])
    def _():
        half = D_MODEL // 2
        xw = xs_ref[...]
        lo = pltpu.bitcast(xw << 16, F32).astype(BF16)
        hi = pltpu.bitcast(xw & jnp.uint32(0xFFFF0000), F32).astype(BF16)
        g = _dot(lo, wg_ref[0, 0:half, :]) + _dot(hi, wg_ref[0, half:, :])
        u = _dot(lo, wu_ref[0, 0:half, :]) + _dot(hi, wu_ref[0, half:, :])
        a = (g * _sigmoid(g) * u).astype(BF16)
        y_ref[...] = _dot(a, wd_ref[0])

    @pl.when(pl.program_id(0) >= na_ref[0])
    def _():
        y_ref[...] = jnp.zeros_like(y_ref)


def _experts(be, na, xs, wg, wu, wd):
    nslots, width = xs.shape
    nblk = nslots // MOE_BLK
    d = D_MODEL
    last = lambda i, na: jnp.minimum(i, na[0] - 1)
    grid_spec = pltpu.PrefetchScalarGridSpec(
        num_scalar_prefetch=2,
        grid=(nblk,),
        in_specs=[
            pl.BlockSpec((MOE_BLK, width), lambda i, be, na: (last(i, na), 0)),
            pl.BlockSpec((1, d, D_EXPERT), lambda i, be, na: (be[last(i, na)], 0, 0)),
            pl.BlockSpec((1, d, D_EXPERT), lambda i, be, na: (be[last(i, na)], 0, 0)),
            pl.BlockSpec((1, D_EXPERT, d), lambda i, be, na: (be[last(i, na)], 0, 0)),
        ],
        out_specs=pl.BlockSpec((MOE_BLK, d), lambda i, be, na: (i, 0)),
    )
    return pl.pallas_call(
        _experts_kernel,
        grid_spec=grid_spec,
        out_shape=jax.ShapeDtypeStruct((nslots, d), F32),
        name="moe_experts",
        compiler_params=_cparams(("arbitrary",)),
    )(be, na, xs, wg, wu, wd)


def _combine_kernel(*refs, final):
    if final:
        dest_ref, rw_ref, y_ref, x_ref, g_ref, fg_ref, o_ref, ybuf, sem = refs
    else:
        dest_ref, rw_ref, y_ref, x_ref, g_ref, o_ref, ybuf, sem = refs
    tm = x_ref.shape[0]

    def issue(i, carry):
        for k in range(2):
            pltpu.make_async_copy(y_ref.at[pl.ds(dest_ref[k, i], 1)], ybuf.at[k, pl.ds(i, 1)], sem).start()
        return carry

    lax.fori_loop(0, tm, issue, 0, unroll=8)

    def drain(i, carry):
        for k in range(2):
            pltpu.make_async_copy(y_ref.at[pl.ds(dest_ref[k, i], 1)], ybuf.at[k, pl.ds(i, 1)], sem).wait()
        return carry

    lax.fori_loop(0, tm, drain, 0, unroll=8)

    w = rw_ref[...]
    eye = lax.broadcasted_iota(I32, (tm, tm), 0) == lax.broadcasted_iota(I32, (tm, tm), 1)
    w0 = jnp.where(eye, w[0:1, :], 0.0).sum(axis=1, keepdims=True)
    w1 = jnp.where(eye, w[1:2, :], 0.0).sum(axis=1, keepdims=True)
    xn = x_ref[...] + g_ref[0] * (w0 * ybuf[0] + w1 * ybuf[1])
    if final:
        ms = jnp.mean(xn * xn, axis=-1, keepdims=True)
        xn = xn * lax.rsqrt(ms + NORM_EPS) * fg_ref[...]
    o_ref[...] = xn


def _combine(dest, rw, y, x, gate, tok_off, final_g):
    b, l, d = x.shape
    tm = min(TM_COMBINE, l)
    assert tok_off % tm == 0 and l % tm == 0
    off = tok_off // tm
    per_b = l // tm
    final = final_g is not None
    in_specs = [
        pl.BlockSpec((2, tm), lambda i: (0, off + i), memory_space=pltpu.SMEM),
        pl.BlockSpec((2, tm), lambda i: (0, off + i)),
        pl.BlockSpec(memory_space=pl.ANY),
        pl.BlockSpec((tm, d), lambda i: (i, 0)),
        pl.BlockSpec((1, 1, d), lambda i: (i // per_b, 0, 0)),
    ]
    args = [dest, rw, y, x.reshape(b * l, d), gate]
    if final:
        in_specs.append(pl.BlockSpec((1, d), lambda i: (0, 0)))
        args.append(final_g.reshape(1, d))
    out = pl.pallas_call(
        functools.partial(_combine_kernel, final=final),
        grid=(b * l // tm,),
        in_specs=in_specs,
        out_specs=pl.BlockSpec((tm, d), lambda i: (i, 0)),
        out_shape=jax.ShapeDtypeStruct((b * l, d), F32),
        scratch_shapes=[pltpu.VMEM((2, tm, d), F32), pltpu.SemaphoreType.DMA(())],
        name="moe_combine_final" if final else "moe_combine",
        compiler_params=_cparams(("arbitrary",)),
    )(*args)
    return out.reshape(b, l, d)


def _permute_w_in(w):
    ck0, cv0, gt0 = 2560, 2688, 2816
    dup = lambda c0: [w[:, c0 + g * HEAD_DIM:c0 + (g + 1) * HEAD_DIM] for g in range(SWA_KV_HEADS) for _ in range(2)]
    cols = [w[:, :ck0]] + dup(ck0) + dup(cv0) + [w[:, gt0:]]
    return jnp.concatenate(cols, axis=1).astype(BF16)


def _rope_tables(seq):
    t = jnp.arange(seq, dtype=I32)
    row = (t // GRID_W).astype(F32)
    col = (t % GRID_W).astype(F32)
    inv = ROPE_THETA ** (-jnp.arange(0, ROPE_AXIS_DIM, 2, dtype=F32) / ROPE_AXIS_DIM)
    ang_r = row[:, None] * inv[None, :]
    ang_c = col[:, None] * inv[None, :]
    cos = jnp.concatenate([jnp.cos(ang_r)] * 2 + [jnp.cos(ang_c)] * 2, axis=1)
    sin = jnp.concatenate([-jnp.sin(ang_r), jnp.sin(ang_r), -jnp.sin(ang_c), jnp.sin(ang_c)], axis=1)
    return jnp.tile(cos, (1, 2)), jnp.tile(sin, (1, 2))


def _router_params(w_rg, b_rg, w_re, b_re):
    d = w_rg.shape[0]
    wr = jnp.zeros((ROUTER_ROWS, d), F32)
    wr = wr.at[0:N_EXPERTS].set(w_re.T).at[N_EXPERTS:N_EXPERTS + N_GROUPS].set(w_rg.T)
    br = jnp.zeros((ROUTER_ROWS,), F32)
    br = br.at[0:N_EXPERTS].set(b_re).at[N_EXPERTS:N_EXPERTS + N_GROUPS].set(b_rg)
    return wr, jnp.broadcast_to(br[:, None], (ROUTER_ROWS, LANES_V7X))


def _moe(re, hp_parts, wg, wu, wd, tri):
    ntok = re.shape[1]
    nblk, _ = _moe_slots(ntok)
    dest, be, na = _plan(re, tri)
    xs = jnp.zeros((nblk * MOE_BLK, D_MODEL // 2), U32)
    tok_off = 0
    for hp in hp_parts:
        xs = _dispatch(dest, hp, xs, tok_off)
        tok_off += hp.shape[0]
    y = _experts(be.reshape(-1), na.reshape(-1)[:1], xs, wg, wu, wd)
    return dest, y


def kernel(x, c, ctx, c_ctx, w_mod, b_mod, norm1_g, norm2_g, w_in, rpb_a, w_pool, pool_scale, sink_c,
           w_branch, w_out, w_router_group, b_router_group, w_router_expert, b_router_expert,
           w_exp_gate, w_exp_up, w_exp_down, final_g):
    b, s, d = x.shape
    lc = ctx.shape[1]
    depth = w_mod.shape[0]
    nz = b * lc

    pad_rows = (-(b + 1)) % 8
    cc = jnp.concatenate([c, c_ctx[None, :], jnp.zeros((pad_rows, d), F32)], axis=0)
    mod = _modvec(cc, w_mod, b_mod)
    cos, sin = _rope_tables(s)
    mask_bias = _swa_mask_bias()
    tri = jnp.asarray(np.triu(np.ones((PLAN_TILE, PLAN_TILE), np.float32), k=1), BF16)

    z = ctx
    for l in range(depth):
        last = l == depth - 1
        mx = [mod[l, :b, j * d:(j + 1) * d].reshape(b, 1, d) for j in range(N_MOD)]
        mz = [jnp.broadcast_to(mod[l, b:b + 1, j * d:(j + 1) * d], (b, d)).reshape(b, 1, d) for j in range(N_MOD)]
        w_perm = _permute_w_in(w_in[l])
        wp = w_pool[l].astype(BF16)
        wb = w_branch[l].astype(BF16)
        wo = w_out[l].astype(BF16)
        wr, br = _router_params(w_router_group[l], b_router_group[l], w_router_expert[l], b_router_expert[l])
        wg, wu, wd = w_exp_gate[l].astype(BF16), w_exp_up[l].astype(BF16), w_exp_down[l].astype(BF16)

        qkv_x, u_x, qkc_x, vc_x, gt_x = _inproj(x, mx[0], mx[1], norm1_g[l], w_perm, cos, sin)
        qkv_z, u_z, qkc_z, vc_z, gt_z = _inproj(z, mz[0], mz[1], norm1_g[l], w_perm, None, None)
        ya = _natten(qkv_x, qkv_z, _natten_bias(rpb_a[l]))
        yb = _pool(u_x, wp, pool_scale[l])
        yc = _swa(qkc_x, vc_x, qkc_z, vc_z, mask_bias, sink_c[l])

        x, hp_x, re, rw = _merge(ya, yb, yc, gt_x, x, wb, wo, mx[2], norm2_g[l], mx[3], mx[4], wr, br)
        hp_parts = [hp_x]
        x_off = 0
        if not last:
            za = _ctx_attn_a(qkv_z)
            zb = _pool(u_z, wp, pool_scale[l])
            zc = _ctx_attn_c(qkc_z, vc_z, sink_c[l])
            z, hp_z, re_z, rw_z = _merge(za, zb, zc, gt_z, z, wb, wo, mz[2], norm2_g[l], mz[3], mz[4], wr, br)
            re = jnp.concatenate([re_z, re], axis=1)
            rw = jnp.concatenate([rw_z, rw], axis=1)
            hp_parts = [hp_z, hp_x]
            x_off = nz

        dest, y = _moe(re, hp_parts, wg, wu, wd, tri)
        if not last:
            z = _combine(dest, rw, y, z, mz[5], 0, None)
        x = _combine(dest, rw, y, x, mx[5], x_off, final_g if last else None)
    return x
```

```python
import functools

import numpy as np
import jax
import jax.numpy as jnp
from jax import lax
from jax.experimental import pallas as pl
from jax.experimental.pallas import tpu as pltpu

F32 = jnp.float32
BF16 = jnp.bfloat16
I32 = jnp.int32
U32 = jnp.uint32

D_MODEL = 1024
GRID_W = 64
HEAD_DIM = 64
ATTN_SCALE = HEAD_DIM ** -0.5
NA_HEADS = 8
WIN_ROWS = 8
WIN_COLS = 16
POOL_WINDOWS = (2, 4, 8, 16)
POOL_GW = 128
SWA_HEADS = 8
SWA_KV_HEADS = 2
SWA_WINDOW = 128
SWA_BLK = 128
ROPE_THETA = 10000.0
ROPE_AXIS_DIM = HEAD_DIM // 2
N_GROUPS = 4
EXP_PER_GROUP = 8
N_EXPERTS = 32
D_EXPERT = 512
N_MOD = 6
NORM_EPS = 1e-6
NEG_INF = -1e30

LANES_V7X = 128
VMEM_LIMIT_V7X = 56 * 1024 * 1024

COL_AQ, COL_AK, COL_AV, COL_BU = 0, 512, 1024, 1536
COL_CQ, COL_CK, COL_CV, COL_GT = 2048, 2560, 2816, 3072
IN_COLS = 6144

TM_PROJ = 512
TM_MERGE = 512
MERGE_SPLIT = 2
POOL_CHUNK = 512
PLAN_TILE = 512
MOE_BLK = 512
EXPERT_SPLIT = 2
TM_DISPATCH = 512
TM_COMBINE = 256
NAT_ROWS_PER_ITER = 8
SWA_BLKS_PER_ITER = 2


def _cparams(sem, vmem=VMEM_LIMIT_V7X):
    return pltpu.CompilerParams(dimension_semantics=sem, vmem_limit_bytes=vmem)


def _dot(a, b):
    return jnp.dot(a, b, preferred_element_type=F32)


def _dot_nt(a, b):
    return lax.dot_general(a, b, (((1,), (1,)), ((), ())), preferred_element_type=F32)


def _modvec_kernel(c_ref, w_ref, b_ref, o_ref):
    c = c_ref[...]
    s = c / (1.0 + jnp.exp(-c))
    o_ref[0] = jnp.dot(s, w_ref[0], preferred_element_type=F32,
                       precision=lax.Precision.HIGHEST) + b_ref[0]


def _modvec(cc, w_mod, b_mod):
    depth, d, n = w_mod.shape
    rows = cc.shape[0]
    tn = 1536
    return pl.pallas_call(
        _modvec_kernel,
        grid=(depth, n // tn),
        in_specs=[
            pl.BlockSpec((rows, d), lambda l, j: (0, 0)),
            pl.BlockSpec((1, d, tn), lambda l, j: (l, 0, j)),
            pl.BlockSpec((1, 1, tn), lambda l, j: (l, 0, j)),
        ],
        out_specs=pl.BlockSpec((1, rows, tn), lambda l, j: (l, 0, j)),
        out_shape=jax.ShapeDtypeStruct((depth, rows, n), F32),
        name="modvec",
        compiler_params=_cparams(("arbitrary", "arbitrary")),
    )(cc, w_mod, b_mod.reshape(depth, 1, n))


def _rope_apply(t, cos, sin, first_half):
    outs = []
    for j in range(t.shape[1] // LANES_V7X):
        tj = t[:, j * LANES_V7X:(j + 1) * LANES_V7X]
        partner = jnp.where(first_half, pltpu.roll(tj, LANES_V7X - 16, 1), pltpu.roll(tj, 16, 1))
        outs.append(tj * cos + partner * sin)
    return outs[0] if len(outs) == 1 else jnp.concatenate(outs, axis=1)


def _inproj_kernel(*refs, rope):
    if rope:
        (x_ref, sh_ref, sc_ref, g_ref, w_ref, cos_ref, sin_ref,
         qkv_ref, u_ref, qkc_ref, vc_ref, gt_ref) = refs
    else:
        (x_ref, sh_ref, sc_ref, g_ref, w_ref,
         qkv_ref, u_ref, qkc_ref, vc_ref, gt_ref) = refs
    x = x_ref[0]
    ms = jnp.mean(x * x, axis=-1, keepdims=True)
    y = x * lax.rsqrt(ms + NORM_EPS) * g_ref[...]
    h = (y * (1.0 + sc_ref[0]) + sh_ref[0]).astype(BF16)

    def proj(a, b):
        return _dot(h, w_ref[:, a:b])

    qkv_ref[0, :, 0:512] = (proj(COL_AQ, COL_AQ + 512) * ATTN_SCALE).astype(BF16)
    qkv_ref[0, :, 512:1024] = proj(COL_AK, COL_AK + 512).astype(BF16)
    qkv_ref[0, :, 1024:1536] = proj(COL_AV, COL_AV + 512).astype(BF16)
    u_ref[0] = proj(COL_BU, COL_BU + 512)
    cq = proj(COL_CQ, COL_CQ + 512)
    ck = proj(COL_CK, COL_CK + 256)
    if rope:
        lane = lax.broadcasted_iota(I32, (x.shape[0], LANES_V7X), 1)
        first_half = (lane % 32) < 16
        cos = cos_ref[...]
        sin = sin_ref[...]
        cq = _rope_apply(cq, cos, sin, first_half)
        ck = _rope_apply(ck, cos, sin, first_half)
    qkc_ref[0, :, 0:512] = (cq * ATTN_SCALE).astype(BF16)
    qkc_ref[0, :, 512:768] = ck.astype(BF16)
    vc_ref[0] = proj(COL_CV, COL_CV + 256).astype(BF16)
    for j in range(6):
        gt_ref[0, :, j * 512:(j + 1) * 512] = proj(COL_GT + j * 512, COL_GT + (j + 1) * 512).astype(BF16)


def _inproj(x, shift, scale, gain, w_perm, cos, sin):
    b, l, d = x.shape
    tm = min(TM_PROJ, l)
    rope = cos is not None
    in_specs = [
        pl.BlockSpec((1, tm, d), lambda bi, i: (bi, i, 0)),
        pl.BlockSpec((1, 1, d), lambda bi, i: (bi, 0, 0)),
        pl.BlockSpec((1, 1, d), lambda bi, i: (bi, 0, 0)),
        pl.BlockSpec((1, d), lambda bi, i: (0, 0)),
        pl.BlockSpec((d, IN_COLS), lambda bi, i: (0, 0), pipeline_mode=pl.Buffered(1)),
    ]
    args = [x, shift, scale, gain.reshape(1, d), w_perm]
    if rope:
        in_specs += [pl.BlockSpec((tm, LANES_V7X), lambda bi, i: (i, 0)),
                     pl.BlockSpec((tm, LANES_V7X), lambda bi, i: (i, 0))]
        args += [cos, sin]
    widths = (1536, 512, 768, 256, 3072)
    dtypes = (BF16, F32, BF16, BF16, BF16)
    return pl.pallas_call(
        functools.partial(_inproj_kernel, rope=rope),
        grid=(b, l // tm),
        in_specs=in_specs,
        out_specs=[pl.BlockSpec((1, tm, w), lambda bi, i: (bi, i, 0)) for w in widths],
        out_shape=[jax.ShapeDtypeStruct((b, l, w), dt) for w, dt in zip(widths, dtypes)],
        name="inproj_rope" if rope else "inproj_ctx",
        compiler_params=_cparams(("parallel", "parallel")),
    )(*args)


def _stack_heads(q2):
    n = q2.shape[0]
    row = lax.broadcasted_iota(I32, (2 * n, LANES_V7X), 0)
    lane = lax.broadcasted_iota(I32, (2 * n, LANES_V7X), 1)
    keep = (row < n) == (lane < HEAD_DIM)
    return jnp.where(keep, jnp.concatenate([q2, q2], axis=0), jnp.zeros((), q2.dtype))


def _unstack_heads(o):
    n = o.shape[0] // 2
    lane = lax.broadcasted_iota(I32, (n, LANES_V7X), 1)
    return jnp.where(lane < HEAD_DIM, o[:n], o[n:])


def _lane_blocks(blocks):
    for s in blocks:
        for j in range(s.shape[1] // LANES_V7X):
            yield s[:, j * LANES_V7X:(j + 1) * LANES_V7X]


def _joint_softmax(score_blocks, extra_logit=None):
    col = None
    for c in _lane_blocks(score_blocks):
        col = c if col is None else jnp.maximum(col, c)
    m = col.max(axis=1, keepdims=True)
    if extra_logit is not None:
        m = jnp.maximum(m, extra_logit)
    probs = [jnp.exp(s - m) for s in score_blocks]
    col = None
    for c in _lane_blocks(probs):
        col = c if col is None else col + c
    denom = col.sum(axis=1, keepdims=True)
    if extra_logit is not None:
        denom = denom + jnp.exp(extra_logit - m)
    return [p.astype(BF16) for p in probs], denom


def _softmax_pv(score_blocks, value_blocks, extra_logit=None):
    probs, denom = _joint_softmax(score_blocks, extra_logit)
    acc = None
    for p, v in zip(probs, value_blocks):
        pv = _dot(p, v)
        acc = pv if acc is None else acc + pv
    return acc / denom


def _natten_kernel(q_ref, k_ref, v_ref, kz_ref, vz_ref, bias_ref, o_ref, *, rows):
    kz = kz_ref[0]
    vz = vz_ref[0]
    win = WIN_ROWS * GRID_W

    nq = 2 * GRID_W
    unroll = NAT_ROWS_PER_ITER

    def body(it, carry):
        tok = [pl.multiple_of((it * unroll + u) * GRID_W, GRID_W) for u in range(unroll)]
        qs = [_stack_heads(q_ref[0, pl.ds(tok[u], GRID_W), :]) for u in range(unroll)]
        s_c_all = _dot_nt(jnp.concatenate(qs, axis=0), kz)
        p_c, o_w, denom = [], [], []
        for u in range(unroll):
            r = it * unroll + u
            r0 = jnp.clip(r - WIN_ROWS // 2, 0, rows - WIN_ROWS)
            cls = r0 - r + (WIN_ROWS - 1)
            k0 = pl.multiple_of(r0 * GRID_W, GRID_W)
            s_w = _dot_nt(qs[u], k_ref[0, pl.ds(k0, win), :]) + bias_ref[0, cls]
            probs, den = _joint_softmax([s_w, s_c_all[u * nq:(u + 1) * nq]])
            o_w.append(_dot(probs[0], v_ref[0, pl.ds(k0, win), :]))
            p_c.append(probs[1])
            denom.append(den)
        o_c_all = _dot(jnp.concatenate(p_c, axis=0), vz)
        for u in range(unroll):
            o = (o_w[u] + o_c_all[u * nq:(u + 1) * nq]) / denom[u]
            o_ref[0, pl.ds(tok[u], GRID_W), :] = _unstack_heads(o).astype(BF16)
        return carry

    lax.fori_loop(0, rows // unroll, body, 0)


def _natten(qkv_x, qkv_z, bias):
    b, s, _ = qkv_x.shape
    lc = qkv_z.shape[1]
    rows = s // GRID_W
    assert rows >= WIN_ROWS and rows % NAT_ROWS_PER_ITER == 0
    npair = NA_HEADS // 2
    blk = lambda off: pl.BlockSpec((1, s, LANES_V7X), lambda hp, bi, off=off: (bi, 0, off + hp))
    blkz = lambda off: pl.BlockSpec((1, lc, LANES_V7X), lambda hp, bi, off=off: (bi, 0, off + hp))
    return pl.pallas_call(
        functools.partial(_natten_kernel, rows=rows),
        grid=(npair, b),
        in_specs=[blk(0), blk(npair), blk(2 * npair), blkz(npair), blkz(2 * npair),
                  pl.BlockSpec((1, WIN_ROWS, 2 * GRID_W, WIN_ROWS * GRID_W), lambda hp, bi: (hp, 0, 0, 0))],
        out_specs=pl.BlockSpec((1, s, LANES_V7X), lambda hp, bi: (bi, 0, hp)),
        out_shape=jax.ShapeDtypeStruct((b, s, NA_HEADS * HEAD_DIM), BF16),
        name="natten",
        compiler_params=_cparams(("parallel", "parallel")),
    )(qkv_x, qkv_x, qkv_x, qkv_z, qkv_z, bias)


def _natten_bias(rpb):
    h = rpb.shape[0]
    qc = np.arange(GRID_W)
    kc = np.arange(GRID_W)
    wstart = np.clip(qc - WIN_COLS // 2, 0, GRID_W - WIN_COLS)
    col_ok = (kc[None, :] >= wstart[:, None]) & (kc[None, :] < wstart[:, None] + WIN_COLS)
    dcol = np.clip(kc[None, :] - qc[:, None] + WIN_COLS - 1, 0, 2 * WIN_COLS - 2)
    e = jnp.where(col_ok[None, None], rpb[:, :, dcol].astype(F32), NEG_INF)
    idx = np.arange(WIN_ROWS)[:, None] + np.arange(WIN_ROWS)[None, :]
    bc = e[:, idx]
    bc = bc.transpose(0, 1, 3, 2, 4).reshape(h, WIN_ROWS, GRID_W, WIN_ROWS * GRID_W)
    bc = bc.reshape(h // 2, 2, WIN_ROWS, GRID_W, WIN_ROWS * GRID_W).transpose(0, 2, 1, 3, 4)
    return bc.reshape(h // 2, WIN_ROWS, 2 * GRID_W, WIN_ROWS * GRID_W)


def _swa_kernel(sink_ref, q_ref, k_ref, v_ref, kz_ref, vz_ref, mask_ref, o_ref, *, seq):
    g = pl.program_id(0)
    kz = kz_ref[0]
    vz = vz_ref[0]
    nwin = 3 * SWA_BLK
    group = SWA_HEADS // SWA_KV_HEADS
    nq = 2 * SWA_BLK
    npairs = group // 2
    unroll = SWA_BLKS_PER_ITER
    row = lax.broadcasted_iota(I32, (nq, 1), 0)
    sinks = [jnp.where(row < SWA_BLK, sink_ref[group * g + 2 * jj], sink_ref[group * g + 2 * jj + 1])
             for jj in range(npairs)]

    def body(it, carry):
        q0s = [pl.multiple_of((it * unroll + u) * SWA_BLK, SWA_BLK) for u in range(unroll)]
        chains = [(u, jj) for u in range(unroll) for jj in range(npairs)]
        qs = [_stack_heads(q_ref[0, pl.ds(q0s[u], SWA_BLK), jj * LANES_V7X:(jj + 1) * LANES_V7X])
              for u, jj in chains]
        s_c_all = _dot_nt(jnp.concatenate(qs, axis=0), kz)
        p_c, o_w, denom = [], [], []
        for c, (u, jj) in enumerate(chains):
            start = pl.multiple_of(jnp.clip(q0s[u] - SWA_BLK, 0, seq - nwin), SWA_BLK)
            var = (q0s[u] - start) // SWA_BLK
            mb = mask_ref[var]
            s_w = _dot_nt(qs[c], k_ref[0, pl.ds(start, nwin), :]) + jnp.concatenate([mb, mb], axis=0)
            probs, den = _joint_softmax([s_w, s_c_all[c * nq:(c + 1) * nq]], extra_logit=sinks[jj])
            o_w.append(_dot(probs[0], v_ref[0, pl.ds(start, nwin), :]))
            p_c.append(probs[1])
            denom.append(den)
        o_c_all = _dot(jnp.concatenate(p_c, axis=0), vz)
        for c, (u, jj) in enumerate(chains):
            o = (o_w[c] + o_c_all[c * nq:(c + 1) * nq]) / denom[c]
            o_ref[0, pl.ds(q0s[u], SWA_BLK), jj * LANES_V7X:(jj + 1) * LANES_V7X] = (
                _unstack_heads(o).astype(BF16))
        return carry

    lax.fori_loop(0, seq // (SWA_BLK * unroll), body, 0)


def _swa(qkc_x, vc_x, qkc_z, vc_z, mask_bias, sink):
    b, s, _ = qkc_x.shape
    lc = qkc_z.shape[1]
    assert s % (SWA_BLK * SWA_BLKS_PER_ITER) == 0 and s >= 3 * SWA_BLK
    qw = 2 * LANES_V7X
    return pl.pallas_call(
        functools.partial(_swa_kernel, seq=s),
        grid=(SWA_KV_HEADS, b),
        in_specs=[
            pl.BlockSpec(memory_space=pltpu.SMEM),
            pl.BlockSpec((1, s, qw), lambda g, bi: (bi, 0, g)),
            pl.BlockSpec((1, s, LANES_V7X), lambda g, bi: (bi, 0, 4 + g)),
            pl.BlockSpec((1, s, LANES_V7X), lambda g, bi: (bi, 0, g)),
            pl.BlockSpec((1, lc, LANES_V7X), lambda g, bi: (bi, 0, 4 + g)),
            pl.BlockSpec((1, lc, LANES_V7X), lambda g, bi: (bi, 0, g)),
            pl.BlockSpec((3, SWA_BLK, 3 * SWA_BLK), lambda g, bi: (0, 0, 0)),
        ],
        out_specs=pl.BlockSpec((1, s, qw), lambda g, bi: (bi, 0, g)),
        out_shape=jax.ShapeDtypeStruct((b, s, SWA_HEADS * HEAD_DIM), BF16),
        name="swa",
        compiler_params=_cparams(("parallel", "parallel")),
    )(sink, qkc_x, qkc_x, vc_x, qkc_z, vc_z, mask_bias)


def _swa_mask_bias():
    i = np.arange(SWA_BLK)[:, None]
    j = np.arange(3 * SWA_BLK)[None, :]
    out = np.zeros((3, SWA_BLK, 3 * SWA_BLK), np.float32)
    for v in range(3):
        rel = j - v * SWA_BLK - i
        out[v] = np.where(np.abs(rel) <= SWA_WINDOW, 0.0, NEG_INF)
    return jnp.asarray(out)


def _ctx_attn_kernel(*refs, ngroups, use_sink):
    if use_sink:
        sink_ref, q_ref, k_ref, v_ref, o_ref = refs
    else:
        q_ref, k_ref, v_ref, o_ref = refs
    g = pl.program_id(0)
    k = k_ref[0]
    v = v_ref[0]
    n = q_ref.shape[1]
    row = lax.broadcasted_iota(I32, (2 * n, 1), 0)
    for jj in range(ngroups):
        qs = _stack_heads(q_ref[0, :, jj * LANES_V7X:(jj + 1) * LANES_V7X])
        s = _dot_nt(qs, k)
        sink = None
        if use_sink:
            base = 2 * ngroups * g + 2 * jj
            sink = jnp.where(row < n, sink_ref[base], sink_ref[base + 1])
        o = _softmax_pv([s], [v], extra_logit=sink)
        o_ref[0, :, jj * LANES_V7X:(jj + 1) * LANES_V7X] = _unstack_heads(o).astype(BF16)


def _ctx_attn_a(qkv_z):
    b, lc, _ = qkv_z.shape
    npair = NA_HEADS // 2
    blk = lambda off: pl.BlockSpec((1, lc, LANES_V7X), lambda hp, bi, off=off: (bi, 0, off + hp))
    return pl.pallas_call(
        functools.partial(_ctx_attn_kernel, ngroups=1, use_sink=False),
        grid=(npair, b),
        in_specs=[blk(0), blk(npair), blk(2 * npair)],
        out_specs=pl.BlockSpec((1, lc, LANES_V7X), lambda hp, bi: (bi, 0, hp)),
        out_shape=jax.ShapeDtypeStruct((b, lc, NA_HEADS * HEAD_DIM), BF16),
        name="ctx_attn_a",
        compiler_params=_cparams(("parallel", "parallel")),
    )(qkv_z, qkv_z, qkv_z)


def _ctx_attn_c(qkc_z, vc_z, sink):
    b, lc, _ = qkc_z.shape
    qw = 2 * LANES_V7X
    return pl.pallas_call(
        functools.partial(_ctx_attn_kernel, ngroups=2, use_sink=True),
        grid=(SWA_KV_HEADS, b),
        in_specs=[
            pl.BlockSpec(memory_space=pltpu.SMEM),
            pl.BlockSpec((1, lc, qw), lambda g, bi: (bi, 0, g)),
            pl.BlockSpec((1, lc, LANES_V7X), lambda g, bi: (bi, 0, 4 + g)),
            pl.BlockSpec((1, lc, LANES_V7X), lambda g, bi: (bi, 0, g)),
        ],
        out_specs=pl.BlockSpec((1, lc, qw), lambda g, bi: (bi, 0, g)),
        out_shape=jax.ShapeDtypeStruct((b, lc, SWA_HEADS * HEAD_DIM), BF16),
        name="ctx_attn_c",
        compiler_params=_cparams(("parallel", "parallel")),
    )(sink, qkc_z, qkc_z, vc_z)


PAD_ROWS = 8


def _pool_kernel(u_ref, wp_ref, ps_ref, o_ref, pad_ref, *, length, chunk):
    zeros = jnp.zeros((PAD_ROWS, POOL_GW), F32)
    pad_ref[0:PAD_ROWS, :] = zeros
    pad_ref[length + PAD_ROWS:length + 2 * PAD_ROWS, :] = zeros
    for g, w in enumerate(POOL_WINDOWS):
        half = w // 2
        assert half <= PAD_ROWS
        lanes = slice(g * POOL_GW, (g + 1) * POOL_GW)
        pad_ref[PAD_ROWS:length + PAD_ROWS, :] = u_ref[0, :, lanes]
        for c in range(length // chunk):
            c0 = c * chunk
            acc = pad_ref[c0 + PAD_ROWS - half:c0 + PAD_ROWS - half + chunk, :]
            for j in range(-half + 1, half):
                acc = acc + pad_ref[c0 + PAD_ROWS + j:c0 + PAD_ROWS + j + chunk, :]
            t = c0 + lax.broadcasted_iota(I32, (chunk, 1), 0)
            cnt = (jnp.minimum(t + half, length) - jnp.maximum(t - half, 0)).astype(F32)
            centre = pad_ref[c0 + PAD_ROWS:c0 + PAD_ROWS + chunk, :]
            d = acc / cnt - centre
            y = _dot(d.astype(BF16), wp_ref[g]) * ps_ref[:, lanes]
            o_ref[0, c0:c0 + chunk, lanes] = y.astype(BF16)


def _pool(u, w_pool, pool_scale):
    b, l, width = u.shape
    chunk = min(POOL_CHUNK, l)
    return pl.pallas_call(
        functools.partial(_pool_kernel, length=l, chunk=chunk),
        grid=(b,),
        in_specs=[
            pl.BlockSpec((1, l, width), lambda bi: (bi, 0, 0)),
            pl.BlockSpec(w_pool.shape, lambda bi: (0, 0, 0)),
            pl.BlockSpec((1, width), lambda bi: (0, 0)),
        ],
        out_specs=pl.BlockSpec((1, l, width), lambda bi: (bi, 0, 0)),
        out_shape=jax.ShapeDtypeStruct((b, l, width), BF16),
        scratch_shapes=[pltpu.VMEM((l + 2 * PAD_ROWS, POOL_GW), F32)],
        name="pool",
        compiler_params=_cparams(("parallel",)),
    )(u, w_pool, pool_scale.reshape(1, width))


ROUTER_COLS = LANES_V7X


def _sigmoid(x):
    return 0.5 * jnp.tanh(0.5 * x) + 0.5


def _merge_kernel(ya_ref, yb_ref, yc_ref, gt_ref, x_ref, wb_ref, wo_ref, g1_ref, n2_ref, sh_ref, sc_ref,
                  wr_ref, br_ref, xo_ref, hp_ref, re_ref, rw_ref):
    tm = x_ref.shape[1]
    sub = tm // MERGE_SPLIT
    for part in range(MERGE_SPLIT):
        _merge_rows(slice(part * sub, (part + 1) * sub), ya_ref, yb_ref, yc_ref, gt_ref, x_ref, wb_ref, wo_ref,
                    g1_ref, n2_ref, sh_ref, sc_ref, wr_ref, br_ref, xo_ref, hp_ref, re_ref, rw_ref)


def _merge_rows(rows, ya_ref, yb_ref, yc_ref, gt_ref, x_ref, wb_ref, wo_ref, g1_ref, n2_ref, sh_ref, sc_ref,
                wr_ref, br_ref, xo_ref, hp_ref, re_ref, rw_ref):
    d = D_MODEL
    m = None
    for j, y_ref in enumerate((ya_ref, yb_ref, yc_ref)):
        gate = _sigmoid(gt_ref[0, rows, j * d:(j + 1) * d].astype(F32))
        term = gate * _dot(y_ref[0, rows, :], wb_ref[j])
        m = term if m is None else m + term
    out = _dot(m.astype(BF16), wo_ref[...])
    xn = x_ref[0, rows, :] + g1_ref[0] * out
    xo_ref[0, rows, :] = xn

    ms = jnp.mean(xn * xn, axis=-1, keepdims=True)
    h = xn * lax.rsqrt(ms + NORM_EPS) * n2_ref[...]
    h = h * (1.0 + sc_ref[0]) + sh_ref[0]

    half = d // 2
    h_hi = h.astype(BF16)
    h_hi32 = h_hi.astype(F32)
    lo = pltpu.bitcast(h_hi32[:, :half], U32)
    hi = pltpu.bitcast(h_hi32[:, half:], U32)
    hp_ref[rows, :] = (lo >> 16) | (hi & jnp.uint32(0xFFFF0000))

    h_lo = (h - h_hi32).astype(BF16)
    cross = _dot(h_hi, wr_ref[...])
    logits = (cross[:, :ROUTER_COLS] + cross[:, ROUTER_COLS:] + _dot(h_lo, wr_ref[:, :ROUTER_COLS])
              + br_ref[...])
    logits = logits.T
    tm = logits.shape[1]
    le = logits[0:N_EXPERTS]
    lg = logits[N_EXPERTS:N_EXPERTS + N_GROUPS]
    gi = lax.broadcasted_iota(I32, (N_GROUPS, tm), 0).astype(F32)
    lg_max = lg.max(axis=0, keepdims=True)
    g_top = jnp.where(lg == lg_max, gi, float(N_GROUPS)).min(axis=0, keepdims=True)
    p_grp = 1.0 / jnp.exp(lg - lg_max).sum(axis=0, keepdims=True)
    ei_int = lax.broadcasted_iota(I32, (N_EXPERTS, tm), 0)
    ei = ei_int.astype(F32)
    eg = lax.shift_right_logical(ei_int, 3).astype(F32)
    lm = jnp.where(eg == g_top, le, NEG_INF)
    m1 = lm.max(axis=0, keepdims=True)
    i1 = jnp.where(lm == m1, ei, float(N_EXPERTS)).min(axis=0, keepdims=True)
    lm2 = jnp.where(ei == i1, NEG_INF, lm)
    m2 = lm2.max(axis=0, keepdims=True)
    i2 = jnp.where(lm2 == m2, ei, float(N_EXPERTS)).min(axis=0, keepdims=True)
    a2 = jnp.exp(m2 - m1)
    w1 = p_grp / (1.0 + a2)
    re_ref[:, rows] = jnp.concatenate([i1, i2], axis=0).astype(I32)
    rw_ref[:, rows] = jnp.concatenate([w1, w1 * a2], axis=0)


def _merge(ya, yb, yc, gates, x, wb, wo, gate1, norm2_g, shift2, scale2, wr, br):
    b, l, d = x.shape
    tm = min(TM_MERGE, l)
    nt = l // tm
    tok = lambda bi, i: (bi, i, 0)
    per_b = lambda bi, i: (bi, 0, 0)
    const2 = lambda bi, i: (0, 0)
    in_specs = [
        pl.BlockSpec((1, tm, 512), tok), pl.BlockSpec((1, tm, 512), tok), pl.BlockSpec((1, tm, 512), tok),
        pl.BlockSpec((1, tm, 3 * d), tok), pl.BlockSpec((1, tm, d), tok),
        pl.BlockSpec((3, 512, d), lambda bi, i: (0, 0, 0)), pl.BlockSpec((d, d), const2),
        pl.BlockSpec((1, 1, d), per_b), pl.BlockSpec((1, d), const2),
        pl.BlockSpec((1, 1, d), per_b), pl.BlockSpec((1, 1, d), per_b),
        pl.BlockSpec((d, 2 * ROUTER_COLS), const2), pl.BlockSpec((1, ROUTER_COLS), const2),
    ]
    args = [ya, yb, yc, gates, x, wb, wo, gate1, norm2_g.reshape(1, d), shift2, scale2, wr, br]
    return pl.pallas_call(
        _merge_kernel,
        grid=(b, nt),
        in_specs=in_specs,
        out_specs=[
            pl.BlockSpec((1, tm, d), tok),
            pl.BlockSpec((tm, d // 2), lambda bi, i: (bi * nt + i, 0)),
            pl.BlockSpec((2, tm), lambda bi, i: (0, bi * nt + i)),
            pl.BlockSpec((2, tm), lambda bi, i: (0, bi * nt + i)),
        ],
        out_shape=[
            jax.ShapeDtypeStruct((b, l, d), F32),
            jax.ShapeDtypeStruct((b * l, d // 2), U32),
            jax.ShapeDtypeStruct((2, b * l), I32),
            jax.ShapeDtypeStruct((2, b * l), F32),
        ],
        name="merge_router",
        compiler_params=_cparams(("arbitrary", "arbitrary")),
    )(*args)


def _plan_kernel(re_ref, tri_ref, dest_ref, be_ref, na_ref, cnt_ref, *, ntok, nblk_pad):
    nt = ntok // PLAN_TILE
    ei = lax.broadcasted_iota(I32, (N_EXPERTS, PLAN_TILE), 0)
    cnt_ref[...] = jnp.zeros_like(cnt_ref)

    def rank_tile(i, carry):
        c0 = pl.multiple_of(i * PLAN_TILE, PLAN_TILE)
        e = re_ref[:, pl.ds(c0, PLAN_TILE)]
        hit1 = ei == e[0:1]
        hit2 = ei == e[1:2]
        onehot = jnp.where(hit1 | hit2, 1.0, 0.0)
        before = _dot(onehot.astype(BF16), tri_ref[...]) + cnt_ref[:, 0:1]
        r1 = jnp.where(hit1, before, 0.0).sum(axis=0, keepdims=True)
        r2 = jnp.where(hit2, before, 0.0).sum(axis=0, keepdims=True)
        dest_ref[:, pl.ds(c0, PLAN_TILE)] = jnp.concatenate([r1, r2], axis=0).astype(I32)
        cnt_ref[...] = cnt_ref[...] + onehot.sum(axis=1, keepdims=True)
        return carry

    lax.fori_loop(0, nt, rank_tile, 0)

    cnt = cnt_ref[...]
    padded = jnp.floor((cnt + (MOE_BLK - 1)) * (1.0 / MOE_BLK)) * MOE_BLK
    sub = lax.broadcasted_iota(I32, (N_EXPERTS, LANES_V7X), 0)
    pstart = jnp.zeros_like(padded)
    for e in range(N_EXPERTS - 1):
        pstart = pstart + jnp.where(sub > e, padded[e:e + 1, :], 0.0)
    pend = pstart + padded
    cnt_ref[...] = pstart

    def dest_tile(i, carry):
        c0 = pl.multiple_of(i * PLAN_TILE, PLAN_TILE)
        e = re_ref[:, pl.ds(c0, PLAN_TILE)]
        ps = cnt_ref[:, 0:1]
        s1 = jnp.where(ei == e[0:1], ps, 0.0).sum(axis=0, keepdims=True)
        s2 = jnp.where(ei == e[1:2], ps, 0.0).sum(axis=0, keepdims=True)
        dest_ref[:, pl.ds(c0, PLAN_TILE)] = (dest_ref[:, pl.ds(c0, PLAN_TILE)]
                                              + jnp.concatenate([s1, s2], axis=0).astype(I32))
        return carry

    lax.fori_loop(0, nt, dest_tile, 0)

    first_row = (lax.broadcasted_iota(I32, (N_EXPERTS, nblk_pad), 1) * MOE_BLK).astype(F32)
    owner = jnp.where(pend[:, 0:1] <= first_row, 1.0, 0.0).sum(axis=0, keepdims=True)
    be_ref[...] = jnp.minimum(owner, float(N_EXPERTS - 1)).astype(I32)
    na_ref[...] = (pend[N_EXPERTS - 1:N_EXPERTS, :] * (1.0 / MOE_BLK)).astype(I32)


def _moe_slots(ntok):
    nblk = -(-(2 * ntok + N_EXPERTS * (MOE_BLK - 1)) // MOE_BLK)
    return nblk, -(-nblk // LANES_V7X) * LANES_V7X


def _plan(re, tri):
    ntok = re.shape[1]
    assert ntok % PLAN_TILE == 0
    _, nblk_pad = _moe_slots(ntok)
    return pl.pallas_call(
        functools.partial(_plan_kernel, ntok=ntok, nblk_pad=nblk_pad),
        out_shape=[
            jax.ShapeDtypeStruct((2, ntok), I32),
            jax.ShapeDtypeStruct((1, nblk_pad), I32),
            jax.ShapeDtypeStruct((1, LANES_V7X), I32),
        ],
        scratch_shapes=[pltpu.VMEM((N_EXPERTS, LANES_V7X), F32)],
        name="moe_plan",
        compiler_params=pltpu.CompilerParams(vmem_limit_bytes=VMEM_LIMIT_V7X),
    )(re, tri)


def _dispatch_kernel(dest_ref, hp_ref, xs_in_ref, xs_ref, sem):
    del xs_in_ref
    tm = hp_ref.shape[0]

    def issue(i, carry):
        for k in range(2):
            pltpu.make_async_copy(hp_ref.at[pl.ds(i, 1)], xs_ref.at[pl.ds(dest_ref[k, i], 1)], sem).start()
        return carry

    lax.fori_loop(0, tm, issue, 0, unroll=8)

    def drain(i, carry):
        for k in range(2):
            pltpu.make_async_copy(hp_ref.at[pl.ds(i, 1)], xs_ref.at[pl.ds(dest_ref[k, i], 1)], sem).wait()
        return carry

    lax.fori_loop(0, tm, drain, 0, unroll=8)


def _dispatch(dest, hp, xs, tok_off):
    ntok, width = hp.shape
    tm = min(TM_DISPATCH, ntok)
    assert tok_off % tm == 0 and ntok % tm == 0
    off = tok_off // tm
    return pl.pallas_call(
        _dispatch_kernel,
        grid=(ntok // tm,),
        in_specs=[
            pl.BlockSpec((2, tm), lambda i: (0, off + i), memory_space=pltpu.SMEM),
            pl.BlockSpec((tm, width), lambda i: (i, 0)),
            pl.BlockSpec(memory_space=pl.ANY),
        ],
        out_specs=pl.BlockSpec(memory_space=pl.ANY),
        out_shape=jax.ShapeDtypeStruct(xs.shape, U32),
        scratch_shapes=[pltpu.SemaphoreType.DMA(())],
        input_output_aliases={2: 0},
        name="moe_dispatch",
        compiler_params=_cparams(("arbitrary",)),
    )(dest, hp, xs)


def _experts_kernel(be_ref, na_ref, xs_ref, wg_ref, wu_ref, wd_ref, y_ref, wgb_ref, wub_ref, wdb_ref):
    i = pl.program_id(0)
    active = i < na_ref[0]
    new_expert = (i == 0) | (be_ref[i] != be_ref[jnp.maximum(i - 1, 0)])

    @pl.when(active & new_expert)
    def _():
        wgb_ref[...] = wg_ref[0].astype(BF16)
        wub_ref[...] = wu_ref[0].astype(BF16)
        wdb_ref[...] = wd_ref[0].astype(BF16)

    @pl.when(active)
    def _():
        half = D_MODEL // 2
        sub = MOE_BLK // EXPERT_SPLIT
        for part in range(EXPERT_SPLIT):
            rows = slice(part * sub, (part + 1) * sub)
            xw = xs_ref[rows, :]
            lo = pltpu.bitcast(xw << 16, F32).astype(BF16)
            hi = pltpu.bitcast(xw & jnp.uint32(0xFFFF0000), F32).astype(BF16)
            g = _dot(lo, wgb_ref[0:half, :]) + _dot(hi, wgb_ref[half:, :])
            u = _dot(lo, wub_ref[0:half, :]) + _dot(hi, wub_ref[half:, :])
            a = (g * _sigmoid(g) * u).astype(BF16)
            y_ref[rows, :] = _dot(a, wdb_ref[...])

    @pl.when(pl.program_id(0) >= na_ref[0])
    def _():
        y_ref[...] = jnp.zeros_like(y_ref)


def _experts(be, na, xs, wg, wu, wd, layer):
    nslots, width = xs.shape
    nblk = nslots // MOE_BLK
    d = D_MODEL
    last = lambda i, na: jnp.minimum(i, na[0] - 1)
    grid_spec = pltpu.PrefetchScalarGridSpec(
        num_scalar_prefetch=2,
        grid=(nblk,),
        in_specs=[
            pl.BlockSpec((MOE_BLK, width), lambda i, be, na: (last(i, na), 0)),
            pl.BlockSpec((None, 1, d, D_EXPERT), lambda i, be, na: (layer, be[last(i, na)], 0, 0)),
            pl.BlockSpec((None, 1, d, D_EXPERT), lambda i, be, na: (layer, be[last(i, na)], 0, 0)),
            pl.BlockSpec((None, 1, D_EXPERT, d), lambda i, be, na: (layer, be[last(i, na)], 0, 0)),
        ],
        out_specs=pl.BlockSpec((MOE_BLK, d), lambda i, be, na: (i, 0)),
        scratch_shapes=[pltpu.VMEM((d, D_EXPERT), BF16), pltpu.VMEM((d, D_EXPERT), BF16),
                        pltpu.VMEM((D_EXPERT, d), BF16)],
    )
    return pl.pallas_call(
        _experts_kernel,
        grid_spec=grid_spec,
        out_shape=jax.ShapeDtypeStruct((nslots, d), F32),
        name="moe_experts",
        compiler_params=_cparams(("arbitrary",)),
    )(be, na, xs, wg, wu, wd)


def _combine_kernel(*refs, final):
    if final:
        dest_ref, rw_ref, y_ref, x_ref, g_ref, fg_ref, o_ref, ybuf, sem = refs
    else:
        dest_ref, rw_ref, y_ref, x_ref, g_ref, o_ref, ybuf, sem = refs
    tm = x_ref.shape[0]

    def issue(i, carry):
        for k in range(2):
            pltpu.make_async_copy(y_ref.at[pl.ds(dest_ref[k, i], 1)], ybuf.at[k, pl.ds(i, 1)], sem).start()
        return carry

    lax.fori_loop(0, tm, issue, 0, unroll=8)

    def drain(i, carry):
        for k in range(2):
            pltpu.make_async_copy(y_ref.at[pl.ds(dest_ref[k, i], 1)], ybuf.at[k, pl.ds(i, 1)], sem).wait()
        return carry

    lax.fori_loop(0, tm, drain, 0, unroll=8)

    w = rw_ref[...]
    eye = lax.broadcasted_iota(I32, (tm, tm), 0) == lax.broadcasted_iota(I32, (tm, tm), 1)
    w0 = jnp.where(eye, w[0:1, :], 0.0).sum(axis=1, keepdims=True)
    w1 = jnp.where(eye, w[1:2, :], 0.0).sum(axis=1, keepdims=True)
    xn = x_ref[...] + g_ref[0] * (w0 * ybuf[0] + w1 * ybuf[1])
    if final:
        ms = jnp.mean(xn * xn, axis=-1, keepdims=True)
        xn = xn * lax.rsqrt(ms + NORM_EPS) * fg_ref[...]
    o_ref[...] = xn


def _combine(dest, rw, y, x, gate, tok_off, final_g):
    b, l, d = x.shape
    tm = min(TM_COMBINE, l)
    assert tok_off % tm == 0 and l % tm == 0
    off = tok_off // tm
    per_b = l // tm
    final = final_g is not None
    in_specs = [
        pl.BlockSpec((2, tm), lambda i: (0, off + i), memory_space=pltpu.SMEM),
        pl.BlockSpec((2, tm), lambda i: (0, off + i)),
        pl.BlockSpec(memory_space=pl.ANY),
        pl.BlockSpec((tm, d), lambda i: (i, 0)),
        pl.BlockSpec((1, 1, d), lambda i: (i // per_b, 0, 0)),
    ]
    args = [dest, rw, y, x.reshape(b * l, d), gate]
    if final:
        in_specs.append(pl.BlockSpec((1, d), lambda i: (0, 0)))
        args.append(final_g.reshape(1, d))
    out = pl.pallas_call(
        functools.partial(_combine_kernel, final=final),
        grid=(b * l // tm,),
        in_specs=in_specs,
        out_specs=pl.BlockSpec((tm, d), lambda i: (i, 0)),
        out_shape=jax.ShapeDtypeStruct((b * l, d), F32),
        scratch_shapes=[pltpu.VMEM((2, tm, d), F32), pltpu.SemaphoreType.DMA(())],
        name="moe_combine_final" if final else "moe_combine",
        compiler_params=_cparams(("arbitrary",)),
    )(*args)
    return out.reshape(b, l, d)


def _permute_w_in(w):
    ck0, cv0, gt0 = 2560, 2688, 2816
    dup = lambda c0: [w[:, c0 + g * HEAD_DIM:c0 + (g + 1) * HEAD_DIM] for g in range(SWA_KV_HEADS) for _ in range(2)]
    cols = [w[:, :ck0]] + dup(ck0) + dup(cv0) + [w[:, gt0:]]
    return jnp.concatenate(cols, axis=1).astype(BF16)


def _rope_tables(seq):
    t = jnp.arange(seq, dtype=I32)
    row = (t // GRID_W).astype(F32)
    col = (t % GRID_W).astype(F32)
    inv = ROPE_THETA ** (-jnp.arange(0, ROPE_AXIS_DIM, 2, dtype=F32) / ROPE_AXIS_DIM)
    ang_r = row[:, None] * inv[None, :]
    ang_c = col[:, None] * inv[None, :]
    cos = jnp.concatenate([jnp.cos(ang_r)] * 2 + [jnp.cos(ang_c)] * 2, axis=1)
    sin = jnp.concatenate([-jnp.sin(ang_r), jnp.sin(ang_r), -jnp.sin(ang_c), jnp.sin(ang_c)], axis=1)
    return jnp.tile(cos, (1, 2)), jnp.tile(sin, (1, 2))


def _router_params(w_rg, b_rg, w_re, b_re):
    d = w_rg.shape[0]
    pad = ROUTER_COLS - N_EXPERTS - N_GROUPS
    w = jnp.concatenate([w_re, w_rg, jnp.zeros((d, pad), F32)], axis=1)
    w_hi = w.astype(BF16)
    w_lo = (w - w_hi.astype(F32)).astype(BF16)
    br = jnp.concatenate([b_re, b_rg, jnp.zeros((pad,), F32)]).reshape(1, ROUTER_COLS)
    return jnp.concatenate([w_hi, w_lo], axis=1), br


def _moe(re, hp_parts, wg, wu, wd, layer, tri):
    ntok = re.shape[1]
    nblk, _ = _moe_slots(ntok)
    dest, be, na = _plan(re, tri)
    xs = jnp.zeros((nblk * MOE_BLK, D_MODEL // 2), U32)
    tok_off = 0
    for hp in hp_parts:
        xs = _dispatch(dest, hp, xs, tok_off)
        tok_off += hp.shape[0]
    y = _experts(be.reshape(-1), na.reshape(-1)[:1], xs, wg, wu, wd, layer)
    return dest, y


def kernel(x, c, ctx, c_ctx, w_mod, b_mod, norm1_g, norm2_g, w_in, rpb_a, w_pool, pool_scale, sink_c,
           w_branch, w_out, w_router_group, b_router_group, w_router_expert, b_router_expert,
           w_exp_gate, w_exp_up, w_exp_down, final_g):
    b, s, d = x.shape
    lc = ctx.shape[1]
    depth = w_mod.shape[0]
    nz = b * lc

    pad_rows = (-(b + 1)) % 8
    cc = jnp.concatenate([c, c_ctx[None, :], jnp.zeros((pad_rows, d), F32)], axis=0)
    mod = _modvec(cc, w_mod, b_mod)
    cos, sin = _rope_tables(s)
    mask_bias = _swa_mask_bias()
    tri = jnp.asarray(np.triu(np.ones((PLAN_TILE, PLAN_TILE), np.float32), k=1), BF16)

    z = ctx
    for l in range(depth):
        last = l == depth - 1
        mx = [mod[l, :b, j * d:(j + 1) * d].reshape(b, 1, d) for j in range(N_MOD)]
        mz = [jnp.broadcast_to(mod[l, b:b + 1, j * d:(j + 1) * d], (b, d)).reshape(b, 1, d) for j in range(N_MOD)]
        w_perm = _permute_w_in(w_in[l])
        wp = w_pool[l].astype(BF16)
        wb = w_branch[l].astype(BF16)
        wo = w_out[l].astype(BF16)
        wr, br = _router_params(w_router_group[l], b_router_group[l], w_router_expert[l], b_router_expert[l])

        qkv_x, u_x, qkc_x, vc_x, gt_x = _inproj(x, mx[0], mx[1], norm1_g[l], w_perm, cos, sin)
        qkv_z, u_z, qkc_z, vc_z, gt_z = _inproj(z, mz[0], mz[1], norm1_g[l], w_perm, None, None)
        ya = _natten(qkv_x, qkv_z, _natten_bias(rpb_a[l]))
        yb = _pool(u_x, wp, pool_scale[l])
        yc = _swa(qkc_x, vc_x, qkc_z, vc_z, mask_bias, sink_c[l])

        x, hp_x, re, rw = _merge(ya, yb, yc, gt_x, x, wb, wo, mx[2], norm2_g[l], mx[3], mx[4], wr, br)
        hp_parts = [hp_x]
        x_off = 0
        if not last:
            za = _ctx_attn_a(qkv_z)
            zb = _pool(u_z, wp, pool_scale[l])
            zc = _ctx_attn_c(qkc_z, vc_z, sink_c[l])
            z, hp_z, re_z, rw_z = _merge(za, zb, zc, gt_z, z, wb, wo, mz[2], norm2_g[l], mz[3], mz[4], wr, br)
            re = jnp.concatenate([re_z, re], axis=1)
            rw = jnp.concatenate([rw_z, rw], axis=1)
            hp_parts = [hp_z, hp_x]
            x_off = nz

        dest, y = _moe(re, hp_parts, w_exp_gate, w_exp_up, w_exp_down, l, tri)
        if not last:
            z = _combine(dest, rw, y, z, mz[5], 0, None)
        x = _combine(dest, rw, y, x, mx[5], x_off, final_g if last else None)
    return x
```

```python
import functools

import numpy as np
import jax
import jax.numpy as jnp
from jax import lax
from jax.experimental import pallas as pl
from jax.experimental.pallas import tpu as pltpu

F32 = jnp.float32
BF16 = jnp.bfloat16
I32 = jnp.int32
U32 = jnp.uint32

D_MODEL = 1024
GRID_W = 64
HEAD_DIM = 64
ATTN_SCALE = HEAD_DIM ** -0.5
NA_HEADS = 8
WIN_ROWS = 8
WIN_COLS = 16
POOL_WINDOWS = (2, 4, 8, 16)
POOL_GW = 128
SWA_HEADS = 8
SWA_KV_HEADS = 2
SWA_WINDOW = 128
SWA_BLK = 128
ROPE_THETA = 10000.0
ROPE_AXIS_DIM = HEAD_DIM // 2
N_GROUPS = 4
EXP_PER_GROUP = 8
N_EXPERTS = 32
D_EXPERT = 512
N_MOD = 6
NORM_EPS = 1e-6
NEG_INF = -1e30

LANES_V7X = 128
VMEM_LIMIT_V7X = 56 * 1024 * 1024

COL_AQ, COL_AK, COL_AV, COL_BU = 0, 512, 1024, 1536
COL_CQ, COL_CK, COL_CV, COL_GT = 2048, 2560, 2816, 3072
IN_COLS = 6144

TM_PROJ = 512
TM_MERGE = 512
MERGE_SPLIT = 2
POOL_CHUNK = 512
PLAN_TILE = 512
MOE_BLK = 512
EXPERT_SPLIT = 2
TM_DISPATCH = 512
TM_COMBINE = 256
XROWS = (D_MODEL // 2) // LANES_V7X
YROWS = D_MODEL // LANES_V7X
NAT_ROWS_PER_ITER = 8
SWA_BLKS_PER_ITER = 2


def _cparams(sem, vmem=VMEM_LIMIT_V7X):
    return pltpu.CompilerParams(dimension_semantics=sem, vmem_limit_bytes=vmem)


def _dot(a, b):
    return jnp.dot(a, b, preferred_element_type=F32)


def _store_row_tiles(ref, lead, first_token, value):
    n, width = value.shape
    r = width // LANES_V7X
    for q in range(r):
        idx = lead + (pl.ds(first_token * r + q, n, stride=r), slice(None))
        ref[idx] = value[:, q * LANES_V7X:(q + 1) * LANES_V7X]


def _load_row_tiles(ref, lead, first_token, n, r):
    return jnp.concatenate(
        [ref[lead + (pl.ds(first_token * r + q, n, stride=r), slice(None))] for q in range(r)], axis=1)


def _dot_nt(a, b):
    return lax.dot_general(a, b, (((1,), (1,)), ((), ())), preferred_element_type=F32)


def _modvec_kernel(c_ref, w_ref, b_ref, o_ref):
    c = c_ref[...]
    s = c / (1.0 + jnp.exp(-c))
    o_ref[0] = jnp.dot(s, w_ref[0], preferred_element_type=F32,
                       precision=lax.Precision.HIGHEST) + b_ref[0]


def _modvec(cc, w_mod, b_mod):
    depth, d, n = w_mod.shape
    rows = cc.shape[0]
    tn = 1536
    return pl.pallas_call(
        _modvec_kernel,
        grid=(depth, n // tn),
        in_specs=[
            pl.BlockSpec((rows, d), lambda l, j: (0, 0)),
            pl.BlockSpec((1, d, tn), lambda l, j: (l, 0, j)),
            pl.BlockSpec((1, 1, tn), lambda l, j: (l, 0, j)),
        ],
        out_specs=pl.BlockSpec((1, rows, tn), lambda l, j: (l, 0, j)),
        out_shape=jax.ShapeDtypeStruct((depth, rows, n), F32),
        name="modvec",
        compiler_params=_cparams(("arbitrary", "arbitrary")),
    )(cc, w_mod, b_mod.reshape(depth, 1, n))


def _rope_apply(t, cos, sin, first_half):
    outs = []
    for j in range(t.shape[1] // LANES_V7X):
        tj = t[:, j * LANES_V7X:(j + 1) * LANES_V7X]
        partner = jnp.where(first_half, pltpu.roll(tj, LANES_V7X - 16, 1), pltpu.roll(tj, 16, 1))
        outs.append(tj * cos + partner * sin)
    return outs[0] if len(outs) == 1 else jnp.concatenate(outs, axis=1)


def _inproj_kernel(*refs, rope):
    if rope:
        (x_ref, sh_ref, sc_ref, g_ref, w_ref, cos_ref, sin_ref,
         qkv_ref, u_ref, qkc_ref, vc_ref, gt_ref) = refs
    else:
        (x_ref, sh_ref, sc_ref, g_ref, w_ref,
         qkv_ref, u_ref, qkc_ref, vc_ref, gt_ref) = refs
    x = x_ref[0]
    ms = jnp.mean(x * x, axis=-1, keepdims=True)
    y = x * lax.rsqrt(ms + NORM_EPS) * g_ref[...]
    h = (y * (1.0 + sc_ref[0]) + sh_ref[0]).astype(BF16)

    def proj(a, b):
        return _dot(h, w_ref[:, a:b])

    qkv_ref[0, :, 0:512] = (proj(COL_AQ, COL_AQ + 512) * ATTN_SCALE).astype(BF16)
    qkv_ref[0, :, 512:1024] = proj(COL_AK, COL_AK + 512).astype(BF16)
    qkv_ref[0, :, 1024:1536] = proj(COL_AV, COL_AV + 512).astype(BF16)
    u_ref[0] = proj(COL_BU, COL_BU + 512)
    cq = proj(COL_CQ, COL_CQ + 512)
    ck = proj(COL_CK, COL_CK + 256)
    if rope:
        lane = lax.broadcasted_iota(I32, (x.shape[0], LANES_V7X), 1)
        first_half = (lane % 32) < 16
        cos = cos_ref[...]
        sin = sin_ref[...]
        cq = _rope_apply(cq, cos, sin, first_half)
        ck = _rope_apply(ck, cos, sin, first_half)
    qkc_ref[0, :, 0:512] = (cq * ATTN_SCALE).astype(BF16)
    qkc_ref[0, :, 512:768] = ck.astype(BF16)
    vc_ref[0] = proj(COL_CV, COL_CV + 256).astype(BF16)
    for j in range(6):
        gt_ref[0, :, j * 512:(j + 1) * 512] = proj(COL_GT + j * 512, COL_GT + (j + 1) * 512).astype(BF16)


def _inproj(x, shift, scale, gain, w_perm, cos, sin):
    b, l, d = x.shape
    tm = min(TM_PROJ, l)
    rope = cos is not None
    in_specs = [
        pl.BlockSpec((1, tm, d), lambda bi, i: (bi, i, 0)),
        pl.BlockSpec((1, 1, d), lambda bi, i: (bi, 0, 0)),
        pl.BlockSpec((1, 1, d), lambda bi, i: (bi, 0, 0)),
        pl.BlockSpec((1, d), lambda bi, i: (0, 0)),
        pl.BlockSpec((d, IN_COLS), lambda bi, i: (0, 0), pipeline_mode=pl.Buffered(1)),
    ]
    args = [x, shift, scale, gain.reshape(1, d), w_perm]
    if rope:
        in_specs += [pl.BlockSpec((tm, LANES_V7X), lambda bi, i: (i, 0)),
                     pl.BlockSpec((tm, LANES_V7X), lambda bi, i: (i, 0))]
        args += [cos, sin]
    widths = (1536, 512, 768, 256, 3072)
    dtypes = (BF16, F32, BF16, BF16, BF16)
    return pl.pallas_call(
        functools.partial(_inproj_kernel, rope=rope),
        grid=(b, l // tm),
        in_specs=in_specs,
        out_specs=[pl.BlockSpec((1, tm, w), lambda bi, i: (bi, i, 0)) for w in widths],
        out_shape=[jax.ShapeDtypeStruct((b, l, w), dt) for w, dt in zip(widths, dtypes)],
        name="inproj_rope" if rope else "inproj_ctx",
        compiler_params=_cparams(("parallel", "parallel")),
    )(*args)


def _stack_heads(q2):
    n = q2.shape[0]
    row = lax.broadcasted_iota(I32, (2 * n, LANES_V7X), 0)
    lane = lax.broadcasted_iota(I32, (2 * n, LANES_V7X), 1)
    keep = (row < n) == (lane < HEAD_DIM)
    return jnp.where(keep, jnp.concatenate([q2, q2], axis=0), jnp.zeros((), q2.dtype))


def _unstack_heads(o):
    n = o.shape[0] // 2
    lane = lax.broadcasted_iota(I32, (n, LANES_V7X), 1)
    return jnp.where(lane < HEAD_DIM, o[:n], o[n:])


def _lane_blocks(blocks):
    for s in blocks:
        for j in range(s.shape[1] // LANES_V7X):
            yield s[:, j * LANES_V7X:(j + 1) * LANES_V7X]


def _joint_softmax(score_blocks, extra_logit=None):
    col = None
    for c in _lane_blocks(score_blocks):
        col = c if col is None else jnp.maximum(col, c)
    m = col.max(axis=1, keepdims=True)
    if extra_logit is not None:
        m = jnp.maximum(m, extra_logit)
    probs = [jnp.exp(s - m) for s in score_blocks]
    col = None
    for c in _lane_blocks(probs):
        col = c if col is None else col + c
    denom = col.sum(axis=1, keepdims=True)
    if extra_logit is not None:
        denom = denom + jnp.exp(extra_logit - m)
    return [p.astype(BF16) for p in probs], denom


def _softmax_pv(score_blocks, value_blocks, extra_logit=None):
    probs, denom = _joint_softmax(score_blocks, extra_logit)
    acc = None
    for p, v in zip(probs, value_blocks):
        pv = _dot(p, v)
        acc = pv if acc is None else acc + pv
    return acc / denom


def _natten_kernel(q_ref, k_ref, v_ref, kz_ref, vz_ref, bias_ref, o_ref, *, rows):
    kz = kz_ref[0]
    vz = vz_ref[0]
    win = WIN_ROWS * GRID_W

    nq = 2 * GRID_W
    unroll = NAT_ROWS_PER_ITER

    def body(it, carry):
        tok = [pl.multiple_of((it * unroll + u) * GRID_W, GRID_W) for u in range(unroll)]
        qs = [_stack_heads(q_ref[0, pl.ds(tok[u], GRID_W), :]) for u in range(unroll)]
        s_c_all = _dot_nt(jnp.concatenate(qs, axis=0), kz)
        p_c, o_w, denom = [], [], []
        for u in range(unroll):
            r = it * unroll + u
            r0 = jnp.clip(r - WIN_ROWS // 2, 0, rows - WIN_ROWS)
            cls = r0 - r + (WIN_ROWS - 1)
            k0 = pl.multiple_of(r0 * GRID_W, GRID_W)
            s_w = _dot_nt(qs[u], k_ref[0, pl.ds(k0, win), :]) + bias_ref[0, cls]
            probs, den = _joint_softmax([s_w, s_c_all[u * nq:(u + 1) * nq]])
            o_w.append(_dot(probs[0], v_ref[0, pl.ds(k0, win), :]))
            p_c.append(probs[1])
            denom.append(den)
        o_c_all = _dot(jnp.concatenate(p_c, axis=0), vz)
        for u in range(unroll):
            o = (o_w[u] + o_c_all[u * nq:(u + 1) * nq]) / denom[u]
            o_ref[0, pl.ds(tok[u], GRID_W), :] = _unstack_heads(o).astype(BF16)
        return carry

    lax.fori_loop(0, rows // unroll, body, 0)


def _natten(qkv_x, qkv_z, bias):
    b, s, _ = qkv_x.shape
    lc = qkv_z.shape[1]
    rows = s // GRID_W
    assert rows >= WIN_ROWS and rows % NAT_ROWS_PER_ITER == 0
    npair = NA_HEADS // 2
    blk = lambda off: pl.BlockSpec((1, s, LANES_V7X), lambda hp, bi, off=off: (bi, 0, off + hp))
    blkz = lambda off: pl.BlockSpec((1, lc, LANES_V7X), lambda hp, bi, off=off: (bi, 0, off + hp))
    return pl.pallas_call(
        functools.partial(_natten_kernel, rows=rows),
        grid=(npair, b),
        in_specs=[blk(0), blk(npair), blk(2 * npair), blkz(npair), blkz(2 * npair),
                  pl.BlockSpec((1, WIN_ROWS, 2 * GRID_W, WIN_ROWS * GRID_W), lambda hp, bi: (hp, 0, 0, 0))],
        out_specs=pl.BlockSpec((1, s, LANES_V7X), lambda hp, bi: (bi, 0, hp)),
        out_shape=jax.ShapeDtypeStruct((b, s, NA_HEADS * HEAD_DIM), BF16),
        name="natten",
        compiler_params=_cparams(("parallel", "parallel")),
    )(qkv_x, qkv_x, qkv_x, qkv_z, qkv_z, bias)


def _natten_bias(rpb):
    h = rpb.shape[0]
    qc = np.arange(GRID_W)
    kc = np.arange(GRID_W)
    wstart = np.clip(qc - WIN_COLS // 2, 0, GRID_W - WIN_COLS)
    col_ok = (kc[None, :] >= wstart[:, None]) & (kc[None, :] < wstart[:, None] + WIN_COLS)
    dcol = np.clip(kc[None, :] - qc[:, None] + WIN_COLS - 1, 0, 2 * WIN_COLS - 2)
    e = jnp.where(col_ok[None, None], rpb[:, :, dcol].astype(F32), NEG_INF)
    idx = np.arange(WIN_ROWS)[:, None] + np.arange(WIN_ROWS)[None, :]
    bc = e[:, idx]
    bc = bc.transpose(0, 1, 3, 2, 4).reshape(h, WIN_ROWS, GRID_W, WIN_ROWS * GRID_W)
    bc = bc.reshape(h // 2, 2, WIN_ROWS, GRID_W, WIN_ROWS * GRID_W).transpose(0, 2, 1, 3, 4)
    return bc.reshape(h // 2, WIN_ROWS, 2 * GRID_W, WIN_ROWS * GRID_W)


def _swa_kernel(sink_ref, q_ref, k_ref, v_ref, kz_ref, vz_ref, mask_ref, o_ref, *, seq):
    g = pl.program_id(0)
    kz = kz_ref[0]
    vz = vz_ref[0]
    nwin = 3 * SWA_BLK
    group = SWA_HEADS // SWA_KV_HEADS
    nq = 2 * SWA_BLK
    npairs = group // 2
    unroll = SWA_BLKS_PER_ITER
    row = lax.broadcasted_iota(I32, (nq, 1), 0)
    sinks = [jnp.where(row < SWA_BLK, sink_ref[group * g + 2 * jj], sink_ref[group * g + 2 * jj + 1])
             for jj in range(npairs)]

    def body(it, carry):
        q0s = [pl.multiple_of((it * unroll + u) * SWA_BLK, SWA_BLK) for u in range(unroll)]
        chains = [(u, jj) for u in range(unroll) for jj in range(npairs)]
        qs = [_stack_heads(q_ref[0, pl.ds(q0s[u], SWA_BLK), jj * LANES_V7X:(jj + 1) * LANES_V7X])
              for u, jj in chains]
        s_c_all = _dot_nt(jnp.concatenate(qs, axis=0), kz)
        p_c, o_w, denom = [], [], []
        for c, (u, jj) in enumerate(chains):
            start = pl.multiple_of(jnp.clip(q0s[u] - SWA_BLK, 0, seq - nwin), SWA_BLK)
            var = (q0s[u] - start) // SWA_BLK
            mb = mask_ref[var]
            s_w = _dot_nt(qs[c], k_ref[0, pl.ds(start, nwin), :]) + jnp.concatenate([mb, mb], axis=0)
            probs, den = _joint_softmax([s_w, s_c_all[c * nq:(c + 1) * nq]], extra_logit=sinks[jj])
            o_w.append(_dot(probs[0], v_ref[0, pl.ds(start, nwin), :]))
            p_c.append(probs[1])
            denom.append(den)
        o_c_all = _dot(jnp.concatenate(p_c, axis=0), vz)
        for c, (u, jj) in enumerate(chains):
            o = (o_w[c] + o_c_all[c * nq:(c + 1) * nq]) / denom[c]
            o_ref[0, pl.ds(q0s[u], SWA_BLK), jj * LANES_V7X:(jj + 1) * LANES_V7X] = (
                _unstack_heads(o).astype(BF16))
        return carry

    lax.fori_loop(0, seq // (SWA_BLK * unroll), body, 0)


def _swa(qkc_x, vc_x, qkc_z, vc_z, mask_bias, sink):
    b, s, _ = qkc_x.shape
    lc = qkc_z.shape[1]
    assert s % (SWA_BLK * SWA_BLKS_PER_ITER) == 0 and s >= 3 * SWA_BLK
    qw = 2 * LANES_V7X
    return pl.pallas_call(
        functools.partial(_swa_kernel, seq=s),
        grid=(SWA_KV_HEADS, b),
        in_specs=[
            pl.BlockSpec(memory_space=pltpu.SMEM),
            pl.BlockSpec((1, s, qw), lambda g, bi: (bi, 0, g)),
            pl.BlockSpec((1, s, LANES_V7X), lambda g, bi: (bi, 0, 4 + g)),
            pl.BlockSpec((1, s, LANES_V7X), lambda g, bi: (bi, 0, g)),
            pl.BlockSpec((1, lc, LANES_V7X), lambda g, bi: (bi, 0, 4 + g)),
            pl.BlockSpec((1, lc, LANES_V7X), lambda g, bi: (bi, 0, g)),
            pl.BlockSpec((3, SWA_BLK, 3 * SWA_BLK), lambda g, bi: (0, 0, 0)),
        ],
        out_specs=pl.BlockSpec((1, s, qw), lambda g, bi: (bi, 0, g)),
        out_shape=jax.ShapeDtypeStruct((b, s, SWA_HEADS * HEAD_DIM), BF16),
        name="swa",
        compiler_params=_cparams(("parallel", "parallel")),
    )(sink, qkc_x, qkc_x, vc_x, qkc_z, vc_z, mask_bias)


def _swa_mask_bias():
    i = np.arange(SWA_BLK)[:, None]
    j = np.arange(3 * SWA_BLK)[None, :]
    out = np.zeros((3, SWA_BLK, 3 * SWA_BLK), np.float32)
    for v in range(3):
        rel = j - v * SWA_BLK - i
        out[v] = np.where(np.abs(rel) <= SWA_WINDOW, 0.0, NEG_INF)
    return jnp.asarray(out)


def _ctx_attn_kernel(*refs, ngroups, use_sink):
    if use_sink:
        sink_ref, q_ref, k_ref, v_ref, o_ref = refs
    else:
        q_ref, k_ref, v_ref, o_ref = refs
    g = pl.program_id(0)
    k = k_ref[0]
    v = v_ref[0]
    n = q_ref.shape[1]
    row = lax.broadcasted_iota(I32, (2 * n, 1), 0)
    for jj in range(ngroups):
        qs = _stack_heads(q_ref[0, :, jj * LANES_V7X:(jj + 1) * LANES_V7X])
        s = _dot_nt(qs, k)
        sink = None
        if use_sink:
            base = 2 * ngroups * g + 2 * jj
            sink = jnp.where(row < n, sink_ref[base], sink_ref[base + 1])
        o = _softmax_pv([s], [v], extra_logit=sink)
        o_ref[0, :, jj * LANES_V7X:(jj + 1) * LANES_V7X] = _unstack_heads(o).astype(BF16)


def _ctx_attn_a(qkv_z):
    b, lc, _ = qkv_z.shape
    npair = NA_HEADS // 2
    blk = lambda off: pl.BlockSpec((1, lc, LANES_V7X), lambda hp, bi, off=off: (bi, 0, off + hp))
    return pl.pallas_call(
        functools.partial(_ctx_attn_kernel, ngroups=1, use_sink=False),
        grid=(npair, b),
        in_specs=[blk(0), blk(npair), blk(2 * npair)],
        out_specs=pl.BlockSpec((1, lc, LANES_V7X), lambda hp, bi: (bi, 0, hp)),
        out_shape=jax.ShapeDtypeStruct((b, lc, NA_HEADS * HEAD_DIM), BF16),
        name="ctx_attn_a",
        compiler_params=_cparams(("parallel", "parallel")),
    )(qkv_z, qkv_z, qkv_z)


def _ctx_attn_c(qkc_z, vc_z, sink):
    b, lc, _ = qkc_z.shape
    qw = 2 * LANES_V7X
    return pl.pallas_call(
        functools.partial(_ctx_attn_kernel, ngroups=2, use_sink=True),
        grid=(SWA_KV_HEADS, b),
        in_specs=[
            pl.BlockSpec(memory_space=pltpu.SMEM),
            pl.BlockSpec((1, lc, qw), lambda g, bi: (bi, 0, g)),
            pl.BlockSpec((1, lc, LANES_V7X), lambda g, bi: (bi, 0, 4 + g)),
            pl.BlockSpec((1, lc, LANES_V7X), lambda g, bi: (bi, 0, g)),
        ],
        out_specs=pl.BlockSpec((1, lc, qw), lambda g, bi: (bi, 0, g)),
        out_shape=jax.ShapeDtypeStruct((b, lc, SWA_HEADS * HEAD_DIM), BF16),
        name="ctx_attn_c",
        compiler_params=_cparams(("parallel", "parallel")),
    )(sink, qkc_z, qkc_z, vc_z)


PAD_ROWS = 8


def _pool_kernel(u_ref, wp_ref, ps_ref, o_ref, pad_ref, *, length, chunk):
    zeros = jnp.zeros((PAD_ROWS, POOL_GW), F32)
    pad_ref[0:PAD_ROWS, :] = zeros
    pad_ref[length + PAD_ROWS:length + 2 * PAD_ROWS, :] = zeros
    for g, w in enumerate(POOL_WINDOWS):
        half = w // 2
        assert half <= PAD_ROWS
        lanes = slice(g * POOL_GW, (g + 1) * POOL_GW)
        pad_ref[PAD_ROWS:length + PAD_ROWS, :] = u_ref[0, :, lanes]
        for c in range(length // chunk):
            c0 = c * chunk
            acc = pad_ref[c0 + PAD_ROWS - half:c0 + PAD_ROWS - half + chunk, :]
            for j in range(-half + 1, half):
                acc = acc + pad_ref[c0 + PAD_ROWS + j:c0 + PAD_ROWS + j + chunk, :]
            t = c0 + lax.broadcasted_iota(I32, (chunk, 1), 0)
            cnt = (jnp.minimum(t + half, length) - jnp.maximum(t - half, 0)).astype(F32)
            centre = pad_ref[c0 + PAD_ROWS:c0 + PAD_ROWS + chunk, :]
            d = acc / cnt - centre
            y = _dot(d.astype(BF16), wp_ref[g]) * ps_ref[:, lanes]
            o_ref[0, c0:c0 + chunk, lanes] = y.astype(BF16)


def _pool(u, w_pool, pool_scale):
    b, l, width = u.shape
    chunk = min(POOL_CHUNK, l)
    return pl.pallas_call(
        functools.partial(_pool_kernel, length=l, chunk=chunk),
        grid=(b,),
        in_specs=[
            pl.BlockSpec((1, l, width), lambda bi: (bi, 0, 0)),
            pl.BlockSpec(w_pool.shape, lambda bi: (0, 0, 0)),
            pl.BlockSpec((1, width), lambda bi: (0, 0)),
        ],
        out_specs=pl.BlockSpec((1, l, width), lambda bi: (bi, 0, 0)),
        out_shape=jax.ShapeDtypeStruct((b, l, width), BF16),
        scratch_shapes=[pltpu.VMEM((l + 2 * PAD_ROWS, POOL_GW), F32)],
        name="pool",
        compiler_params=_cparams(("parallel",)),
    )(u, w_pool, pool_scale.reshape(1, width))


ROUTER_COLS = LANES_V7X


def _sigmoid(x):
    return 0.5 * jnp.tanh(0.5 * x) + 0.5


def _merge_kernel(ya_ref, yb_ref, yc_ref, gt_ref, x_ref, wb_ref, wo_ref, g1_ref, n2_ref, sh_ref, sc_ref,
                  wr_ref, br_ref, xo_ref, hp_ref, re_ref, rw_ref):
    tm = x_ref.shape[1]
    sub = tm // MERGE_SPLIT
    for part in range(MERGE_SPLIT):
        _merge_rows(slice(part * sub, (part + 1) * sub), ya_ref, yb_ref, yc_ref, gt_ref, x_ref, wb_ref, wo_ref,
                    g1_ref, n2_ref, sh_ref, sc_ref, wr_ref, br_ref, xo_ref, hp_ref, re_ref, rw_ref)


def _merge_rows(rows, ya_ref, yb_ref, yc_ref, gt_ref, x_ref, wb_ref, wo_ref, g1_ref, n2_ref, sh_ref, sc_ref,
                wr_ref, br_ref, xo_ref, hp_ref, re_ref, rw_ref):
    d = D_MODEL
    m = None
    for j, y_ref in enumerate((ya_ref, yb_ref, yc_ref)):
        gate = _sigmoid(gt_ref[0, rows, j * d:(j + 1) * d].astype(F32))
        term = gate * _dot(y_ref[0, rows, :], wb_ref[j])
        m = term if m is None else m + term
    out = _dot(m.astype(BF16), wo_ref[...])
    xn = x_ref[0, rows, :] + g1_ref[0] * out
    xo_ref[0, rows, :] = xn

    ms = jnp.mean(xn * xn, axis=-1, keepdims=True)
    h = xn * lax.rsqrt(ms + NORM_EPS) * n2_ref[...]
    h = h * (1.0 + sc_ref[0]) + sh_ref[0]

    half = d // 2
    h_hi = h.astype(BF16)
    h_hi32 = h_hi.astype(F32)
    lo = pltpu.bitcast(h_hi32[:, :half], U32)
    hi = pltpu.bitcast(h_hi32[:, half:], U32)
    _store_row_tiles(hp_ref, (), rows.start, (lo >> 16) | (hi & jnp.uint32(0xFFFF0000)))

    h_lo = (h - h_hi32).astype(BF16)
    cross = _dot(h_hi, wr_ref[...])
    logits = (cross[:, :ROUTER_COLS] + cross[:, ROUTER_COLS:] + _dot(h_lo, wr_ref[:, :ROUTER_COLS])
              + br_ref[...])
    logits = logits.T
    tm = logits.shape[1]
    le = logits[0:N_EXPERTS]
    lg = logits[N_EXPERTS:N_EXPERTS + N_GROUPS]
    gi = lax.broadcasted_iota(I32, (N_GROUPS, tm), 0).astype(F32)
    lg_max = lg.max(axis=0, keepdims=True)
    g_top = jnp.where(lg == lg_max, gi, float(N_GROUPS)).min(axis=0, keepdims=True)
    p_grp = 1.0 / jnp.exp(lg - lg_max).sum(axis=0, keepdims=True)
    ei_int = lax.broadcasted_iota(I32, (N_EXPERTS, tm), 0)
    ei = ei_int.astype(F32)
    eg = lax.shift_right_logical(ei_int, 3).astype(F32)
    lm = jnp.where(eg == g_top, le, NEG_INF)
    m1 = lm.max(axis=0, keepdims=True)
    i1 = jnp.where(lm == m1, ei, float(N_EXPERTS)).min(axis=0, keepdims=True)
    lm2 = jnp.where(ei == i1, NEG_INF, lm)
    m2 = lm2.max(axis=0, keepdims=True)
    i2 = jnp.where(lm2 == m2, ei, float(N_EXPERTS)).min(axis=0, keepdims=True)
    a2 = jnp.exp(m2 - m1)
    w1 = p_grp / (1.0 + a2)
    re_ref[:, rows] = jnp.concatenate([i1, i2], axis=0).astype(I32)
    rw_ref[:, rows] = jnp.concatenate([w1, w1 * a2], axis=0)


def _merge(ya, yb, yc, gates, x, wb, wo, gate1, norm2_g, shift2, scale2, wr, br):
    b, l, d = x.shape
    tm = min(TM_MERGE, l)
    nt = l // tm
    tok = lambda bi, i: (bi, i, 0)
    per_b = lambda bi, i: (bi, 0, 0)
    const2 = lambda bi, i: (0, 0)
    in_specs = [
        pl.BlockSpec((1, tm, 512), tok), pl.BlockSpec((1, tm, 512), tok), pl.BlockSpec((1, tm, 512), tok),
        pl.BlockSpec((1, tm, 3 * d), tok), pl.BlockSpec((1, tm, d), tok),
        pl.BlockSpec((3, 512, d), lambda bi, i: (0, 0, 0)), pl.BlockSpec((d, d), const2),
        pl.BlockSpec((1, 1, d), per_b), pl.BlockSpec((1, d), const2),
        pl.BlockSpec((1, 1, d), per_b), pl.BlockSpec((1, 1, d), per_b),
        pl.BlockSpec((d, 2 * ROUTER_COLS), const2), pl.BlockSpec((1, ROUTER_COLS), const2),
    ]
    args = [ya, yb, yc, gates, x, wb, wo, gate1, norm2_g.reshape(1, d), shift2, scale2, wr, br]
    return pl.pallas_call(
        _merge_kernel,
        grid=(b, nt),
        in_specs=in_specs,
        out_specs=[
            pl.BlockSpec((1, tm, d), tok),
            pl.BlockSpec((tm * XROWS, LANES_V7X), lambda bi, i: (bi * nt + i, 0)),
            pl.BlockSpec((2, tm), lambda bi, i: (0, bi * nt + i)),
            pl.BlockSpec((2, tm), lambda bi, i: (0, bi * nt + i)),
        ],
        out_shape=[
            jax.ShapeDtypeStruct((b, l, d), F32),
            jax.ShapeDtypeStruct((b * l * XROWS, LANES_V7X), U32),
            jax.ShapeDtypeStruct((2, b * l), I32),
            jax.ShapeDtypeStruct((2, b * l), F32),
        ],
        name="merge_router",
        compiler_params=_cparams(("arbitrary", "arbitrary")),
    )(*args)


def _plan_kernel(re_ref, tri_ref, dest_ref, be_ref, na_ref, cnt_ref, *, ntok, nblk_pad):
    nt = ntok // PLAN_TILE
    ei = lax.broadcasted_iota(I32, (N_EXPERTS, PLAN_TILE), 0)
    cnt_ref[...] = jnp.zeros_like(cnt_ref)

    def rank_tile(i, carry):
        c0 = pl.multiple_of(i * PLAN_TILE, PLAN_TILE)
        e = re_ref[:, pl.ds(c0, PLAN_TILE)]
        hit1 = ei == e[0:1]
        hit2 = ei == e[1:2]
        onehot = jnp.where(hit1 | hit2, 1.0, 0.0)
        before = _dot(onehot.astype(BF16), tri_ref[...]) + cnt_ref[:, 0:1]
        r1 = jnp.where(hit1, before, 0.0).sum(axis=0, keepdims=True)
        r2 = jnp.where(hit2, before, 0.0).sum(axis=0, keepdims=True)
        dest_ref[:, pl.ds(c0, PLAN_TILE)] = jnp.concatenate([r1, r2], axis=0).astype(I32)
        cnt_ref[...] = cnt_ref[...] + onehot.sum(axis=1, keepdims=True)
        return carry

    lax.fori_loop(0, nt, rank_tile, 0)

    cnt = cnt_ref[...]
    padded = jnp.floor((cnt + (MOE_BLK - 1)) * (1.0 / MOE_BLK)) * MOE_BLK
    sub = lax.broadcasted_iota(I32, (N_EXPERTS, LANES_V7X), 0)
    pstart = jnp.zeros_like(padded)
    for e in range(N_EXPERTS - 1):
        pstart = pstart + jnp.where(sub > e, padded[e:e + 1, :], 0.0)
    pend = pstart + padded
    cnt_ref[...] = pstart

    def dest_tile(i, carry):
        c0 = pl.multiple_of(i * PLAN_TILE, PLAN_TILE)
        e = re_ref[:, pl.ds(c0, PLAN_TILE)]
        ps = cnt_ref[:, 0:1]
        s1 = jnp.where(ei == e[0:1], ps, 0.0).sum(axis=0, keepdims=True)
        s2 = jnp.where(ei == e[1:2], ps, 0.0).sum(axis=0, keepdims=True)
        dest_ref[:, pl.ds(c0, PLAN_TILE)] = (dest_ref[:, pl.ds(c0, PLAN_TILE)]
                                              + jnp.concatenate([s1, s2], axis=0).astype(I32))
        return carry

    lax.fori_loop(0, nt, dest_tile, 0)

    first_row = (lax.broadcasted_iota(I32, (N_EXPERTS, nblk_pad), 1) * MOE_BLK).astype(F32)
    owner = jnp.where(pend[:, 0:1] <= first_row, 1.0, 0.0).sum(axis=0, keepdims=True)
    be_ref[...] = jnp.minimum(owner, float(N_EXPERTS - 1)).astype(I32)
    na_ref[...] = (pend[N_EXPERTS - 1:N_EXPERTS, :] * (1.0 / MOE_BLK)).astype(I32)


def _moe_slots(ntok):
    nblk = -(-(2 * ntok + N_EXPERTS * (MOE_BLK - 1)) // MOE_BLK)
    return nblk, -(-nblk // LANES_V7X) * LANES_V7X


def _plan(re, tri):
    ntok = re.shape[1]
    assert ntok % PLAN_TILE == 0
    _, nblk_pad = _moe_slots(ntok)
    return pl.pallas_call(
        functools.partial(_plan_kernel, ntok=ntok, nblk_pad=nblk_pad),
        out_shape=[
            jax.ShapeDtypeStruct((2, ntok), I32),
            jax.ShapeDtypeStruct((1, nblk_pad), I32),
            jax.ShapeDtypeStruct((1, LANES_V7X), I32),
        ],
        scratch_shapes=[pltpu.VMEM((N_EXPERTS, LANES_V7X), F32)],
        name="moe_plan",
        compiler_params=pltpu.CompilerParams(vmem_limit_bytes=VMEM_LIMIT_V7X),
    )(re, tri)


def _dispatch_kernel(dest_ref, hp_ref, xs_in_ref, xs_ref, sem):
    del xs_in_ref
    tm = hp_ref.shape[0] // XROWS

    def row_copy(i, k):
        src = hp_ref.at[pl.ds(pl.multiple_of(i * XROWS, XROWS), XROWS)]
        dst = xs_ref.at[pl.ds(pl.multiple_of(dest_ref[k, i] * XROWS, XROWS), XROWS)]
        return pltpu.make_async_copy(src, dst, sem)

    def issue(i, carry):
        for k in range(2):
            row_copy(i, k).start(priority=k)
        return carry

    lax.fori_loop(0, tm, issue, 0, unroll=8)

    def drain(i, carry):
        for k in range(2):
            row_copy(i, k).wait()
        return carry

    lax.fori_loop(0, tm, drain, 0, unroll=8)


def _dispatch(dest, hp, xs, tok_off):
    ntok, width = hp.shape[0] // XROWS, hp.shape[1]
    tm = min(TM_DISPATCH, ntok)
    assert tok_off % tm == 0 and ntok % tm == 0
    off = tok_off // tm
    return pl.pallas_call(
        _dispatch_kernel,
        grid=(ntok // tm,),
        in_specs=[
            pl.BlockSpec((2, tm), lambda i: (0, off + i), memory_space=pltpu.SMEM),
            pl.BlockSpec((tm * XROWS, width), lambda i: (i, 0)),
            pl.BlockSpec(memory_space=pl.ANY),
        ],
        out_specs=pl.BlockSpec(memory_space=pl.ANY),
        out_shape=jax.ShapeDtypeStruct(xs.shape, U32),
        scratch_shapes=[pltpu.SemaphoreType.DMA(())],
        input_output_aliases={2: 0},
        name="moe_dispatch",
        compiler_params=_cparams(("arbitrary",)),
    )(dest, hp, xs)


def _experts_kernel(be_ref, na_ref, xs_ref, wg_ref, wu_ref, wd_ref, y_ref, wgb_ref, wub_ref, wdb_ref):
    i = pl.program_id(0)
    active = i < na_ref[0]
    new_expert = (i == 0) | (be_ref[i] != be_ref[jnp.maximum(i - 1, 0)])

    @pl.when(active & new_expert)
    def _():
        wgb_ref[...] = wg_ref[0].astype(BF16)
        wub_ref[...] = wu_ref[0].astype(BF16)
        wdb_ref[...] = wd_ref[0].astype(BF16)

    @pl.when(active)
    def _():
        half = D_MODEL // 2
        sub = MOE_BLK // EXPERT_SPLIT
        for part in range(EXPERT_SPLIT):
            xw = _load_row_tiles(xs_ref, (), part * sub, sub, XROWS)
            lo = pltpu.bitcast(xw << 16, F32).astype(BF16)
            hi = pltpu.bitcast(xw & jnp.uint32(0xFFFF0000), F32).astype(BF16)
            g = _dot(lo, wgb_ref[0:half, :]) + _dot(hi, wgb_ref[half:, :])
            u = _dot(lo, wub_ref[0:half, :]) + _dot(hi, wub_ref[half:, :])
            a = (g * _sigmoid(g) * u).astype(BF16)
            _store_row_tiles(y_ref, (), part * sub, _dot(a, wdb_ref[...]))

    @pl.when(pl.program_id(0) >= na_ref[0])
    def _():
        y_ref[...] = jnp.zeros_like(y_ref)


def _experts(be, na, xs, wg, wu, wd, layer):
    nslots, width = xs.shape[0] // XROWS, xs.shape[1]
    nblk = nslots // MOE_BLK
    d = D_MODEL
    last = lambda i, na: jnp.minimum(i, na[0] - 1)
    grid_spec = pltpu.PrefetchScalarGridSpec(
        num_scalar_prefetch=2,
        grid=(nblk,),
        in_specs=[
            pl.BlockSpec((MOE_BLK * XROWS, width), lambda i, be, na: (last(i, na), 0)),
            pl.BlockSpec((None, 1, d, D_EXPERT), lambda i, be, na: (layer, be[last(i, na)], 0, 0)),
            pl.BlockSpec((None, 1, d, D_EXPERT), lambda i, be, na: (layer, be[last(i, na)], 0, 0)),
            pl.BlockSpec((None, 1, D_EXPERT, d), lambda i, be, na: (layer, be[last(i, na)], 0, 0)),
        ],
        out_specs=pl.BlockSpec((MOE_BLK * YROWS, LANES_V7X), lambda i, be, na: (i, 0)),
        scratch_shapes=[pltpu.VMEM((d, D_EXPERT), BF16), pltpu.VMEM((d, D_EXPERT), BF16),
                        pltpu.VMEM((D_EXPERT, d), BF16)],
    )
    return pl.pallas_call(
        _experts_kernel,
        grid_spec=grid_spec,
        out_shape=jax.ShapeDtypeStruct((nslots * YROWS, LANES_V7X), F32),
        name="moe_experts",
        compiler_params=_cparams(("arbitrary",)),
    )(be, na, xs, wg, wu, wd)


def _combine_kernel(*refs, final):
    if final:
        dest_ref, rw_ref, y_ref, x_ref, g_ref, fg_ref, o_ref, ybuf, sem = refs
    else:
        dest_ref, rw_ref, y_ref, x_ref, g_ref, o_ref, ybuf, sem = refs
    tm = x_ref.shape[0]

    def row_copy(i, k):
        src = y_ref.at[pl.ds(pl.multiple_of(dest_ref[k, i] * YROWS, YROWS), YROWS)]
        dst = ybuf.at[k, pl.ds(pl.multiple_of(i * YROWS, YROWS), YROWS)]
        return pltpu.make_async_copy(src, dst, sem)

    def issue(i, carry):
        for k in range(2):
            row_copy(i, k).start(priority=k)
        return carry

    lax.fori_loop(0, tm, issue, 0, unroll=8)

    def drain(i, carry):
        for k in range(2):
            row_copy(i, k).wait()
        return carry

    lax.fori_loop(0, tm, drain, 0, unroll=8)

    w = rw_ref[...]
    eye = lax.broadcasted_iota(I32, (tm, tm), 0) == lax.broadcasted_iota(I32, (tm, tm), 1)
    w0 = jnp.where(eye, w[0:1, :], 0.0).sum(axis=1, keepdims=True)
    w1 = jnp.where(eye, w[1:2, :], 0.0).sum(axis=1, keepdims=True)
    y0 = _load_row_tiles(ybuf, (0,), 0, tm, YROWS)
    y1 = _load_row_tiles(ybuf, (1,), 0, tm, YROWS)
    xn = x_ref[...] + g_ref[0] * (w0 * y0 + w1 * y1)
    if final:
        ms = jnp.mean(xn * xn, axis=-1, keepdims=True)
        xn = xn * lax.rsqrt(ms + NORM_EPS) * fg_ref[...]
    o_ref[...] = xn


def _combine(dest, rw, y, x, gate, tok_off, final_g):
    b, l, d = x.shape
    tm = min(TM_COMBINE, l)
    assert tok_off % tm == 0 and l % tm == 0
    off = tok_off // tm
    per_b = l // tm
    final = final_g is not None
    in_specs = [
        pl.BlockSpec((2, tm), lambda i: (0, off + i), memory_space=pltpu.SMEM),
        pl.BlockSpec((2, tm), lambda i: (0, off + i)),
        pl.BlockSpec(memory_space=pl.ANY),
        pl.BlockSpec((tm, d), lambda i: (i, 0)),
        pl.BlockSpec((1, 1, d), lambda i: (i // per_b, 0, 0)),
    ]
    args = [dest, rw, y, x.reshape(b * l, d), gate]
    if final:
        in_specs.append(pl.BlockSpec((1, d), lambda i: (0, 0)))
        args.append(final_g.reshape(1, d))
    out = pl.pallas_call(
        functools.partial(_combine_kernel, final=final),
        grid=(b * l // tm,),
        in_specs=in_specs,
        out_specs=pl.BlockSpec((tm, d), lambda i: (i, 0)),
        out_shape=jax.ShapeDtypeStruct((b * l, d), F32),
        scratch_shapes=[pltpu.VMEM((2, tm * YROWS, LANES_V7X), F32), pltpu.SemaphoreType.DMA(())],
        name="moe_combine_final" if final else "moe_combine",
        compiler_params=_cparams(("arbitrary",)),
    )(*args)
    return out.reshape(b, l, d)


def _permute_w_in(w):
    ck0, cv0, gt0 = 2560, 2688, 2816
    dup = lambda c0: [w[:, c0 + g * HEAD_DIM:c0 + (g + 1) * HEAD_DIM] for g in range(SWA_KV_HEADS) for _ in range(2)]
    cols = [w[:, :ck0]] + dup(ck0) + dup(cv0) + [w[:, gt0:]]
    return jnp.concatenate(cols, axis=1).astype(BF16)


def _rope_tables(seq):
    t = jnp.arange(seq, dtype=I32)
    row = (t // GRID_W).astype(F32)
    col = (t % GRID_W).astype(F32)
    inv = ROPE_THETA ** (-jnp.arange(0, ROPE_AXIS_DIM, 2, dtype=F32) / ROPE_AXIS_DIM)
    ang_r = row[:, None] * inv[None, :]
    ang_c = col[:, None] * inv[None, :]
    cos = jnp.concatenate([jnp.cos(ang_r)] * 2 + [jnp.cos(ang_c)] * 2, axis=1)
    sin = jnp.concatenate([-jnp.sin(ang_r), jnp.sin(ang_r), -jnp.sin(ang_c), jnp.sin(ang_c)], axis=1)
    return jnp.tile(cos, (1, 2)), jnp.tile(sin, (1, 2))


def _router_params(w_rg, b_rg, w_re, b_re):
    d = w_rg.shape[0]
    pad = ROUTER_COLS - N_EXPERTS - N_GROUPS
    w = jnp.concatenate([w_re, w_rg, jnp.zeros((d, pad), F32)], axis=1)
    w_hi = w.astype(BF16)
    w_lo = (w - w_hi.astype(F32)).astype(BF16)
    br = jnp.concatenate([b_re, b_rg, jnp.zeros((pad,), F32)]).reshape(1, ROUTER_COLS)
    return jnp.concatenate([w_hi, w_lo], axis=1), br


def _moe(re, hp_parts, wg, wu, wd, layer, tri):
    ntok = re.shape[1]
    nblk, _ = _moe_slots(ntok)
    dest, be, na = _plan(re, tri)
    xs = jnp.zeros((nblk * MOE_BLK * XROWS, LANES_V7X), U32)
    tok_off = 0
    for hp in hp_parts:
        xs = _dispatch(dest, hp, xs, tok_off)
        tok_off += hp.shape[0] // XROWS
    y = _experts(be.reshape(-1), na.reshape(-1)[:1], xs, wg, wu, wd, layer)
    return dest, y


def kernel(x, c, ctx, c_ctx, w_mod, b_mod, norm1_g, norm2_g, w_in, rpb_a, w_pool, pool_scale, sink_c,
           w_branch, w_out, w_router_group, b_router_group, w_router_expert, b_router_expert,
           w_exp_gate, w_exp_up, w_exp_down, final_g):
    b, s, d = x.shape
    lc = ctx.shape[1]
    depth = w_mod.shape[0]
    nz = b * lc

    pad_rows = (-(b + 1)) % 8
    cc = jnp.concatenate([c, c_ctx[None, :], jnp.zeros((pad_rows, d), F32)], axis=0)
    mod = _modvec(cc, w_mod, b_mod)
    cos, sin = _rope_tables(s)
    mask_bias = _swa_mask_bias()
    tri = jnp.asarray(np.triu(np.ones((PLAN_TILE, PLAN_TILE), np.float32), k=1), BF16)

    z = ctx
    for l in range(depth):
        last = l == depth - 1
        mx = [mod[l, :b, j * d:(j + 1) * d].reshape(b, 1, d) for j in range(N_MOD)]
        mz = [jnp.broadcast_to(mod[l, b:b + 1, j * d:(j + 1) * d], (b, d)).reshape(b, 1, d) for j in range(N_MOD)]
        w_perm = _permute_w_in(w_in[l])
        wp = w_pool[l].astype(BF16)
        wb = w_branch[l].astype(BF16)
        wo = w_out[l].astype(BF16)
        wr, br = _router_params(w_router_group[l], b_router_group[l], w_router_expert[l], b_router_expert[l])

        qkv_x, u_x, qkc_x, vc_x, gt_x = _inproj(x, mx[0], mx[1], norm1_g[l], w_perm, cos, sin)
        qkv_z, u_z, qkc_z, vc_z, gt_z = _inproj(z, mz[0], mz[1], norm1_g[l], w_perm, None, None)
        ya = _natten(qkv_x, qkv_z, _natten_bias(rpb_a[l]))
        yb = _pool(u_x, wp, pool_scale[l])
        yc = _swa(qkc_x, vc_x, qkc_z, vc_z, mask_bias, sink_c[l])

        x, hp_x, re, rw = _merge(ya, yb, yc, gt_x, x, wb, wo, mx[2], norm2_g[l], mx[3], mx[4], wr, br)
        hp_parts = [hp_x]
        x_off = 0
        if not last:
            za = _ctx_attn_a(qkv_z)
            zb = _pool(u_z, wp, pool_scale[l])
            zc = _ctx_attn_c(qkc_z, vc_z, sink_c[l])
            z, hp_z, re_z, rw_z = _merge(za, zb, zc, gt_z, z, wb, wo, mz[2], norm2_g[l], mz[3], mz[4], wr, br)
            re = jnp.concatenate([re_z, re], axis=1)
            rw = jnp.concatenate([rw_z, rw], axis=1)
            hp_parts = [hp_z, hp_x]
            x_off = nz

        dest, y = _moe(re, hp_parts, w_exp_gate, w_exp_up, w_exp_down, l, tri)
        if not last:
            z = _combine(dest, rw, y, z, mz[5], 0, None)
        x = _combine(dest, rw, y, x, mx[5], x_off, final_g if last else None)
    return x
```

```python
import functools

import numpy as np
import jax
import jax.numpy as jnp
from jax import lax
from jax.experimental import pallas as pl
from jax.experimental.pallas import tpu as pltpu

F32 = jnp.float32
BF16 = jnp.bfloat16
I32 = jnp.int32
U32 = jnp.uint32

D_MODEL = 1024
GRID_W = 64
HEAD_DIM = 64
ATTN_SCALE = HEAD_DIM ** -0.5
NA_HEADS = 8
WIN_ROWS = 8
WIN_COLS = 16
POOL_WINDOWS = (2, 4, 8, 16)
POOL_GW = 128
SWA_HEADS = 8
SWA_KV_HEADS = 2
SWA_WINDOW = 128
SWA_BLK = 128
ROPE_THETA = 10000.0
ROPE_AXIS_DIM = HEAD_DIM // 2
N_GROUPS = 4
EXP_PER_GROUP = 8
N_EXPERTS = 32
D_EXPERT = 512
N_MOD = 6
NORM_EPS = 1e-6
NEG_INF = -1e30

LANES_V7X = 128
VMEM_LIMIT_V7X = 56 * 1024 * 1024

COL_AQ, COL_AK, COL_AV, COL_BU = 0, 512, 1024, 1536
COL_CQ, COL_CK, COL_CV, COL_GT = 2048, 2560, 2816, 3072
IN_COLS = 6144

TM_PROJ = 512
TM_MERGE = 512
MERGE_SPLIT = 2
POOL_CHUNK = 512
PLAN_TILE = 512
MOE_BLK = 512
EXPERT_SPLIT = 2
TM_DISPATCH = 512
TM_COMBINE = 256
XROWS = (D_MODEL // 2) // LANES_V7X
YROWS = D_MODEL // LANES_V7X
NAT_ROWS_PER_ITER = 16
SWA_BLKS_PER_ITER = 4


def _cparams(sem, vmem=VMEM_LIMIT_V7X):
    return pltpu.CompilerParams(dimension_semantics=sem, vmem_limit_bytes=vmem)


def _dot(a, b):
    return jnp.dot(a, b, preferred_element_type=F32)


def _store_row_tiles(ref, lead, first_token, value):
    n, width = value.shape
    r = width // LANES_V7X
    for q in range(r):
        idx = lead + (pl.ds(first_token * r + q, n, stride=r), slice(None))
        ref[idx] = value[:, q * LANES_V7X:(q + 1) * LANES_V7X]


def _load_row_tiles(ref, lead, first_token, n, r):
    return jnp.concatenate(
        [ref[lead + (pl.ds(first_token * r + q, n, stride=r), slice(None))] for q in range(r)], axis=1)


def _dot_nt(a, b):
    return lax.dot_general(a, b, (((1,), (1,)), ((), ())), preferred_element_type=F32)


def _modvec_kernel(c_ref, w_ref, b_ref, o_ref):
    c = c_ref[...]
    s = c / (1.0 + jnp.exp(-c))
    o_ref[0] = jnp.dot(s, w_ref[0], preferred_element_type=F32,
                       precision=lax.Precision.HIGHEST) + b_ref[0]


def _modvec(cc, w_mod, b_mod):
    depth, d, n = w_mod.shape
    rows = cc.shape[0]
    tn = 1536
    return pl.pallas_call(
        _modvec_kernel,
        grid=(depth, n // tn),
        in_specs=[
            pl.BlockSpec((rows, d), lambda l, j: (0, 0)),
            pl.BlockSpec((1, d, tn), lambda l, j: (l, 0, j)),
            pl.BlockSpec((1, 1, tn), lambda l, j: (l, 0, j)),
        ],
        out_specs=pl.BlockSpec((1, rows, tn), lambda l, j: (l, 0, j)),
        out_shape=jax.ShapeDtypeStruct((depth, rows, n), F32),
        name="modvec",
        compiler_params=_cparams(("arbitrary", "arbitrary")),
    )(cc, w_mod, b_mod.reshape(depth, 1, n))


def _rope_apply(t, cos, sin, first_half):
    outs = []
    for j in range(t.shape[1] // LANES_V7X):
        tj = t[:, j * LANES_V7X:(j + 1) * LANES_V7X]
        partner = jnp.where(first_half, pltpu.roll(tj, LANES_V7X - 16, 1), pltpu.roll(tj, 16, 1))
        outs.append(tj * cos + partner * sin)
    return outs[0] if len(outs) == 1 else jnp.concatenate(outs, axis=1)


def _inproj_kernel(*refs, rope):
    if rope:
        (x_ref, sh_ref, sc_ref, g_ref, w_ref, cos_ref, sin_ref,
         qkv_ref, u_ref, qkc_ref, vc_ref, gt_ref) = refs
    else:
        (x_ref, sh_ref, sc_ref, g_ref, w_ref,
         qkv_ref, u_ref, qkc_ref, vc_ref, gt_ref) = refs
    x = x_ref[0]
    ms = jnp.mean(x * x, axis=-1, keepdims=True)
    y = x * lax.rsqrt(ms + NORM_EPS) * g_ref[...]
    h = (y * (1.0 + sc_ref[0]) + sh_ref[0]).astype(BF16)

    def proj(a, b):
        return _dot(h, w_ref[:, a:b])

    qkv_ref[0, :, 0:512] = (proj(COL_AQ, COL_AQ + 512) * ATTN_SCALE).astype(BF16)
    qkv_ref[0, :, 512:1024] = proj(COL_AK, COL_AK + 512).astype(BF16)
    qkv_ref[0, :, 1024:1536] = proj(COL_AV, COL_AV + 512).astype(BF16)
    u_ref[0] = proj(COL_BU, COL_BU + 512)
    cq = proj(COL_CQ, COL_CQ + 512)
    ck = proj(COL_CK, COL_CK + 256)
    if rope:
        lane = lax.broadcasted_iota(I32, (x.shape[0], LANES_V7X), 1)
        first_half = (lane % 32) < 16
        cos = cos_ref[...]
        sin = sin_ref[...]
        cq = _rope_apply(cq, cos, sin, first_half)
        ck = _rope_apply(ck, cos, sin, first_half)
    qkc_ref[0, :, 0:512] = (cq * ATTN_SCALE).astype(BF16)
    qkc_ref[0, :, 512:768] = ck.astype(BF16)
    vc_ref[0] = proj(COL_CV, COL_CV + 256).astype(BF16)
    for j in range(6):
        gt_ref[0, :, j * 512:(j + 1) * 512] = proj(COL_GT + j * 512, COL_GT + (j + 1) * 512).astype(BF16)


def _inproj(x, shift, scale, gain, w_perm, cos, sin):
    b, l, d = x.shape
    tm = min(TM_PROJ, l)
    rope = cos is not None
    in_specs = [
        pl.BlockSpec((1, tm, d), lambda bi, i: (bi, i, 0)),
        pl.BlockSpec((1, 1, d), lambda bi, i: (bi, 0, 0)),
        pl.BlockSpec((1, 1, d), lambda bi, i: (bi, 0, 0)),
        pl.BlockSpec((1, d), lambda bi, i: (0, 0)),
        pl.BlockSpec((d, IN_COLS), lambda bi, i: (0, 0), pipeline_mode=pl.Buffered(1)),
    ]
    args = [x, shift, scale, gain.reshape(1, d), w_perm]
    if rope:
        in_specs += [pl.BlockSpec((tm, LANES_V7X), lambda bi, i: (i, 0)),
                     pl.BlockSpec((tm, LANES_V7X), lambda bi, i: (i, 0))]
        args += [cos, sin]
    widths = (1536, 512, 768, 256, 3072)
    dtypes = (BF16, F32, BF16, BF16, BF16)
    return pl.pallas_call(
        functools.partial(_inproj_kernel, rope=rope),
        grid=(b, l // tm),
        in_specs=in_specs,
        out_specs=[pl.BlockSpec((1, tm, w), lambda bi, i: (bi, i, 0)) for w in widths],
        out_shape=[jax.ShapeDtypeStruct((b, l, w), dt) for w, dt in zip(widths, dtypes)],
        name="inproj_rope" if rope else "inproj_ctx",
        compiler_params=_cparams(("parallel", "parallel")),
    )(*args)


def _stack_heads(q2):
    n = q2.shape[0]
    row = lax.broadcasted_iota(I32, (2 * n, LANES_V7X), 0)
    lane = lax.broadcasted_iota(I32, (2 * n, LANES_V7X), 1)
    keep = (row < n) == (lane < HEAD_DIM)
    return jnp.where(keep, jnp.concatenate([q2, q2], axis=0), jnp.zeros((), q2.dtype))


def _unstack_heads(o):
    n = o.shape[0] // 2
    lane = lax.broadcasted_iota(I32, (n, LANES_V7X), 1)
    return jnp.where(lane < HEAD_DIM, o[:n], o[n:])


def _lane_blocks(blocks):
    for s in blocks:
        for j in range(s.shape[1] // LANES_V7X):
            yield s[:, j * LANES_V7X:(j + 1) * LANES_V7X]


def _joint_softmax(score_blocks, extra_logit=None):
    col = None
    for c in _lane_blocks(score_blocks):
        col = c if col is None else jnp.maximum(col, c)
    m = col.max(axis=1, keepdims=True)
    if extra_logit is not None:
        m = jnp.maximum(m, extra_logit)
    probs = [jnp.exp(s - m) for s in score_blocks]
    col = None
    for c in _lane_blocks(probs):
        col = c if col is None else col + c
    denom = col.sum(axis=1, keepdims=True)
    if extra_logit is not None:
        denom = denom + jnp.exp(extra_logit - m)
    return [p.astype(BF16) for p in probs], denom


def _softmax_pv(score_blocks, value_blocks, extra_logit=None):
    probs, denom = _joint_softmax(score_blocks, extra_logit)
    acc = None
    for p, v in zip(probs, value_blocks):
        pv = _dot(p, v)
        acc = pv if acc is None else acc + pv
    return acc / denom


def _natten_kernel(q_ref, k_ref, v_ref, kz_ref, vz_ref, bias_ref, o_ref, *, rows):
    kz = kz_ref[0]
    vz = vz_ref[0]
    win = WIN_ROWS * GRID_W

    nq = 2 * GRID_W
    unroll = NAT_ROWS_PER_ITER

    def body(it, carry):
        tok = [pl.multiple_of((it * unroll + u) * GRID_W, GRID_W) for u in range(unroll)]
        qs = [_stack_heads(q_ref[0, pl.ds(tok[u], GRID_W), :]) for u in range(unroll)]
        s_c_all = _dot_nt(jnp.concatenate(qs, axis=0), kz)
        p_c, o_w, denom = [], [], []
        for u in range(unroll):
            r = it * unroll + u
            r0 = jnp.clip(r - WIN_ROWS // 2, 0, rows - WIN_ROWS)
            cls = r0 - r + (WIN_ROWS - 1)
            k0 = pl.multiple_of(r0 * GRID_W, GRID_W)
            s_w = _dot_nt(qs[u], k_ref[0, pl.ds(k0, win), :]) + bias_ref[0, cls]
            probs, den = _joint_softmax([s_w, s_c_all[u * nq:(u + 1) * nq]])
            o_w.append(_dot(probs[0], v_ref[0, pl.ds(k0, win), :]))
            p_c.append(probs[1])
            denom.append(den)
        o_c_all = _dot(jnp.concatenate(p_c, axis=0), vz)
        for u in range(unroll):
            o = (o_w[u] + o_c_all[u * nq:(u + 1) * nq]) / denom[u]
            o_ref[0, pl.ds(tok[u], GRID_W), :] = _unstack_heads(o).astype(BF16)
        return carry

    lax.fori_loop(0, rows // unroll, body, 0)


def _natten(qkv_x, qkv_z, bias):
    b, s, _ = qkv_x.shape
    lc = qkv_z.shape[1]
    rows = s // GRID_W
    assert rows >= WIN_ROWS and rows % NAT_ROWS_PER_ITER == 0
    npair = NA_HEADS // 2
    blk = lambda off: pl.BlockSpec((1, s, LANES_V7X), lambda hp, bi, off=off: (bi, 0, off + hp))
    blkz = lambda off: pl.BlockSpec((1, lc, LANES_V7X), lambda hp, bi, off=off: (bi, 0, off + hp))
    return pl.pallas_call(
        functools.partial(_natten_kernel, rows=rows),
        grid=(npair, b),
        in_specs=[blk(0), blk(npair), blk(2 * npair), blkz(npair), blkz(2 * npair),
                  pl.BlockSpec((1, WIN_ROWS, 2 * GRID_W, WIN_ROWS * GRID_W), lambda hp, bi: (hp, 0, 0, 0))],
        out_specs=pl.BlockSpec((1, s, LANES_V7X), lambda hp, bi: (bi, 0, hp)),
        out_shape=jax.ShapeDtypeStruct((b, s, NA_HEADS * HEAD_DIM), BF16),
        name="natten",
        compiler_params=_cparams(("parallel", "parallel")),
    )(qkv_x, qkv_x, qkv_x, qkv_z, qkv_z, bias)


def _natten_bias(rpb):
    h = rpb.shape[0]
    qc = np.arange(GRID_W)
    kc = np.arange(GRID_W)
    wstart = np.clip(qc - WIN_COLS // 2, 0, GRID_W - WIN_COLS)
    col_ok = (kc[None, :] >= wstart[:, None]) & (kc[None, :] < wstart[:, None] + WIN_COLS)
    dcol = np.clip(kc[None, :] - qc[:, None] + WIN_COLS - 1, 0, 2 * WIN_COLS - 2)
    e = jnp.where(col_ok[None, None], rpb[:, :, dcol].astype(F32), NEG_INF)
    idx = np.arange(WIN_ROWS)[:, None] + np.arange(WIN_ROWS)[None, :]
    bc = e[:, idx]
    bc = bc.transpose(0, 1, 3, 2, 4).reshape(h, WIN_ROWS, GRID_W, WIN_ROWS * GRID_W)
    bc = bc.reshape(h // 2, 2, WIN_ROWS, GRID_W, WIN_ROWS * GRID_W).transpose(0, 2, 1, 3, 4)
    return bc.reshape(h // 2, WIN_ROWS, 2 * GRID_W, WIN_ROWS * GRID_W)


def _swa_kernel(sink_ref, q_ref, k_ref, v_ref, kz_ref, vz_ref, mask_ref, o_ref, *, seq):
    g = pl.program_id(0)
    kz = kz_ref[0]
    vz = vz_ref[0]
    nwin = 3 * SWA_BLK
    group = SWA_HEADS // SWA_KV_HEADS
    nq = 2 * SWA_BLK
    npairs = group // 2
    unroll = SWA_BLKS_PER_ITER
    row = lax.broadcasted_iota(I32, (nq, 1), 0)
    sinks = [jnp.where(row < SWA_BLK, sink_ref[group * g + 2 * jj], sink_ref[group * g + 2 * jj + 1])
             for jj in range(npairs)]

    def body(it, carry):
        q0s = [pl.multiple_of((it * unroll + u) * SWA_BLK, SWA_BLK) for u in range(unroll)]
        chains = [(u, jj) for u in range(unroll) for jj in range(npairs)]
        qs = [_stack_heads(q_ref[0, pl.ds(q0s[u], SWA_BLK), jj * LANES_V7X:(jj + 1) * LANES_V7X])
              for u, jj in chains]
        s_c_all = _dot_nt(jnp.concatenate(qs, axis=0), kz)
        p_c, o_w, denom = [], [], []
        for c, (u, jj) in enumerate(chains):
            start = pl.multiple_of(jnp.clip(q0s[u] - SWA_BLK, 0, seq - nwin), SWA_BLK)
            var = (q0s[u] - start) // SWA_BLK
            mb = mask_ref[var]
            s_w = _dot_nt(qs[c], k_ref[0, pl.ds(start, nwin), :]) + jnp.concatenate([mb, mb], axis=0)
            probs, den = _joint_softmax([s_w, s_c_all[c * nq:(c + 1) * nq]], extra_logit=sinks[jj])
            o_w.append(_dot(probs[0], v_ref[0, pl.ds(start, nwin), :]))
            p_c.append(probs[1])
            denom.append(den)
        o_c_all = _dot(jnp.concatenate(p_c, axis=0), vz)
        for c, (u, jj) in enumerate(chains):
            o = (o_w[c] + o_c_all[c * nq:(c + 1) * nq]) / denom[c]
            o_ref[0, pl.ds(q0s[u], SWA_BLK), jj * LANES_V7X:(jj + 1) * LANES_V7X] = (
                _unstack_heads(o).astype(BF16))
        return carry

    lax.fori_loop(0, seq // (SWA_BLK * unroll), body, 0)


def _swa(qkc_x, vc_x, qkc_z, vc_z, mask_bias, sink):
    b, s, _ = qkc_x.shape
    lc = qkc_z.shape[1]
    assert s % (SWA_BLK * SWA_BLKS_PER_ITER) == 0 and s >= 3 * SWA_BLK
    qw = 2 * LANES_V7X
    return pl.pallas_call(
        functools.partial(_swa_kernel, seq=s),
        grid=(SWA_KV_HEADS, b),
        in_specs=[
            pl.BlockSpec(memory_space=pltpu.SMEM),
            pl.BlockSpec((1, s, qw), lambda g, bi: (bi, 0, g)),
            pl.BlockSpec((1, s, LANES_V7X), lambda g, bi: (bi, 0, 4 + g)),
            pl.BlockSpec((1, s, LANES_V7X), lambda g, bi: (bi, 0, g)),
            pl.BlockSpec((1, lc, LANES_V7X), lambda g, bi: (bi, 0, 4 + g)),
            pl.BlockSpec((1, lc, LANES_V7X), lambda g, bi: (bi, 0, g)),
            pl.BlockSpec((3, SWA_BLK, 3 * SWA_BLK), lambda g, bi: (0, 0, 0)),
        ],
        out_specs=pl.BlockSpec((1, s, qw), lambda g, bi: (bi, 0, g)),
        out_shape=jax.ShapeDtypeStruct((b, s, SWA_HEADS * HEAD_DIM), BF16),
        name="swa",
        compiler_params=_cparams(("parallel", "parallel")),
    )(sink, qkc_x, qkc_x, vc_x, qkc_z, vc_z, mask_bias)


def _swa_mask_bias():
    i = np.arange(SWA_BLK)[:, None]
    j = np.arange(3 * SWA_BLK)[None, :]
    out = np.zeros((3, SWA_BLK, 3 * SWA_BLK), np.float32)
    for v in range(3):
        rel = j - v * SWA_BLK - i
        out[v] = np.where(np.abs(rel) <= SWA_WINDOW, 0.0, NEG_INF)
    return jnp.asarray(out)


def _ctx_attn_kernel(*refs, ngroups, use_sink):
    if use_sink:
        sink_ref, q_ref, k_ref, v_ref, o_ref = refs
    else:
        q_ref, k_ref, v_ref, o_ref = refs
    g = pl.program_id(0)
    k = k_ref[0]
    v = v_ref[0]
    n = q_ref.shape[1]
    row = lax.broadcasted_iota(I32, (2 * n, 1), 0)
    for jj in range(ngroups):
        qs = _stack_heads(q_ref[0, :, jj * LANES_V7X:(jj + 1) * LANES_V7X])
        s = _dot_nt(qs, k)
        sink = None
        if use_sink:
            base = 2 * ngroups * g + 2 * jj
            sink = jnp.where(row < n, sink_ref[base], sink_ref[base + 1])
        o = _softmax_pv([s], [v], extra_logit=sink)
        o_ref[0, :, jj * LANES_V7X:(jj + 1) * LANES_V7X] = _unstack_heads(o).astype(BF16)


def _ctx_attn_a(qkv_z):
    b, lc, _ = qkv_z.shape
    npair = NA_HEADS // 2
    blk = lambda off: pl.BlockSpec((1, lc, LANES_V7X), lambda hp, bi, off=off: (bi, 0, off + hp))
    return pl.pallas_call(
        functools.partial(_ctx_attn_kernel, ngroups=1, use_sink=False),
        grid=(npair, b),
        in_specs=[blk(0), blk(npair), blk(2 * npair)],
        out_specs=pl.BlockSpec((1, lc, LANES_V7X), lambda hp, bi: (bi, 0, hp)),
        out_shape=jax.ShapeDtypeStruct((b, lc, NA_HEADS * HEAD_DIM), BF16),
        name="ctx_attn_a",
        compiler_params=_cparams(("parallel", "parallel")),
    )(qkv_z, qkv_z, qkv_z)


def _ctx_attn_c(qkc_z, vc_z, sink):
    b, lc, _ = qkc_z.shape
    qw = 2 * LANES_V7X
    return pl.pallas_call(
        functools.partial(_ctx_attn_kernel, ngroups=2, use_sink=True),
        grid=(SWA_KV_HEADS, b),
        in_specs=[
            pl.BlockSpec(memory_space=pltpu.SMEM),
            pl.BlockSpec((1, lc, qw), lambda g, bi: (bi, 0, g)),
            pl.BlockSpec((1, lc, LANES_V7X), lambda g, bi: (bi, 0, 4 + g)),
            pl.BlockSpec((1, lc, LANES_V7X), lambda g, bi: (bi, 0, g)),
        ],
        out_specs=pl.BlockSpec((1, lc, qw), lambda g, bi: (bi, 0, g)),
        out_shape=jax.ShapeDtypeStruct((b, lc, SWA_HEADS * HEAD_DIM), BF16),
        name="ctx_attn_c",
        compiler_params=_cparams(("parallel", "parallel")),
    )(sink, qkc_z, qkc_z, vc_z)


PAD_ROWS = 8


def _pool_kernel(u_ref, wp_ref, ps_ref, o_ref, pad_ref, *, length, chunk):
    zeros = jnp.zeros((PAD_ROWS, POOL_GW), F32)
    pad_ref[0:PAD_ROWS, :] = zeros
    pad_ref[length + PAD_ROWS:length + 2 * PAD_ROWS, :] = zeros
    for g, w in enumerate(POOL_WINDOWS):
        half = w // 2
        assert half <= PAD_ROWS
        lanes = slice(g * POOL_GW, (g + 1) * POOL_GW)
        pad_ref[PAD_ROWS:length + PAD_ROWS, :] = u_ref[0, :, lanes]
        for c in range(length // chunk):
            c0 = c * chunk
            acc = pad_ref[c0 + PAD_ROWS - half:c0 + PAD_ROWS - half + chunk, :]
            for j in range(-half + 1, half):
                acc = acc + pad_ref[c0 + PAD_ROWS + j:c0 + PAD_ROWS + j + chunk, :]
            t = c0 + lax.broadcasted_iota(I32, (chunk, 1), 0)
            cnt = (jnp.minimum(t + half, length) - jnp.maximum(t - half, 0)).astype(F32)
            centre = pad_ref[c0 + PAD_ROWS:c0 + PAD_ROWS + chunk, :]
            d = acc / cnt - centre
            y = _dot(d.astype(BF16), wp_ref[g]) * ps_ref[:, lanes]
            o_ref[0, c0:c0 + chunk, lanes] = y.astype(BF16)


def _pool(u, w_pool, pool_scale):
    b, l, width = u.shape
    chunk = min(POOL_CHUNK, l)
    return pl.pallas_call(
        functools.partial(_pool_kernel, length=l, chunk=chunk),
        grid=(b,),
        in_specs=[
            pl.BlockSpec((1, l, width), lambda bi: (bi, 0, 0)),
            pl.BlockSpec(w_pool.shape, lambda bi: (0, 0, 0)),
            pl.BlockSpec((1, width), lambda bi: (0, 0)),
        ],
        out_specs=pl.BlockSpec((1, l, width), lambda bi: (bi, 0, 0)),
        out_shape=jax.ShapeDtypeStruct((b, l, width), BF16),
        scratch_shapes=[pltpu.VMEM((l + 2 * PAD_ROWS, POOL_GW), F32)],
        name="pool",
        compiler_params=_cparams(("parallel",)),
    )(u, w_pool, pool_scale.reshape(1, width))


ROUTER_COLS = LANES_V7X


def _sigmoid(x):
    return 0.5 * jnp.tanh(0.5 * x) + 0.5


def _merge_kernel(ya_ref, yb_ref, yc_ref, gt_ref, x_ref, wb_ref, wo_ref, g1_ref, n2_ref, sh_ref, sc_ref,
                  wr_ref, br_ref, xo_ref, hp_ref, re_ref, rw_ref):
    tm = x_ref.shape[1]
    sub = tm // MERGE_SPLIT
    for part in range(MERGE_SPLIT):
        _merge_rows(slice(part * sub, (part + 1) * sub), ya_ref, yb_ref, yc_ref, gt_ref, x_ref, wb_ref, wo_ref,
                    g1_ref, n2_ref, sh_ref, sc_ref, wr_ref, br_ref, xo_ref, hp_ref, re_ref, rw_ref)


def _merge_rows(rows, ya_ref, yb_ref, yc_ref, gt_ref, x_ref, wb_ref, wo_ref, g1_ref, n2_ref, sh_ref, sc_ref,
                wr_ref, br_ref, xo_ref, hp_ref, re_ref, rw_ref):
    d = D_MODEL
    m = None
    for j, y_ref in enumerate((ya_ref, yb_ref, yc_ref)):
        gate = _sigmoid(gt_ref[0, rows, j * d:(j + 1) * d].astype(F32))
        term = gate * _dot(y_ref[0, rows, :], wb_ref[j])
        m = term if m is None else m + term
    out = _dot(m.astype(BF16), wo_ref[...])
    xn = x_ref[0, rows, :] + g1_ref[0] * out
    xo_ref[0, rows, :] = xn

    ms = jnp.mean(xn * xn, axis=-1, keepdims=True)
    h = xn * lax.rsqrt(ms + NORM_EPS) * n2_ref[...]
    h = h * (1.0 + sc_ref[0]) + sh_ref[0]

    half = d // 2
    h_hi = h.astype(BF16)
    h_hi32 = h_hi.astype(F32)
    lo = pltpu.bitcast(h_hi32[:, :half], U32)
    hi = pltpu.bitcast(h_hi32[:, half:], U32)
    _store_row_tiles(hp_ref, (), rows.start, (lo >> 16) | (hi & jnp.uint32(0xFFFF0000)))

    h_lo = (h - h_hi32).astype(BF16)
    cross = _dot(h_hi, wr_ref[...])
    logits = (cross[:, :ROUTER_COLS] + cross[:, ROUTER_COLS:] + _dot(h_lo, wr_ref[:, :ROUTER_COLS])
              + br_ref[...])
    logits = logits.T
    tm = logits.shape[1]
    le = logits[0:N_EXPERTS]
    lg = logits[N_EXPERTS:N_EXPERTS + N_GROUPS]
    gi = lax.broadcasted_iota(I32, (N_GROUPS, tm), 0).astype(F32)
    lg_max = lg.max(axis=0, keepdims=True)
    g_top = jnp.where(lg == lg_max, gi, float(N_GROUPS)).min(axis=0, keepdims=True)
    p_grp = 1.0 / jnp.exp(lg - lg_max).sum(axis=0, keepdims=True)
    ei_int = lax.broadcasted_iota(I32, (N_EXPERTS, tm), 0)
    ei = ei_int.astype(F32)
    eg = lax.shift_right_logical(ei_int, 3).astype(F32)
    lm = jnp.where(eg == g_top, le, NEG_INF)
    m1 = lm.max(axis=0, keepdims=True)
    i1 = jnp.where(lm == m1, ei, float(N_EXPERTS)).min(axis=0, keepdims=True)
    lm2 = jnp.where(ei == i1, NEG_INF, lm)
    m2 = lm2.max(axis=0, keepdims=True)
    i2 = jnp.where(lm2 == m2, ei, float(N_EXPERTS)).min(axis=0, keepdims=True)
    a2 = jnp.exp(m2 - m1)
    w1 = p_grp / (1.0 + a2)
    re_ref[:, rows] = jnp.concatenate([i1, i2], axis=0).astype(I32)
    rw_ref[:, rows] = jnp.concatenate([w1, w1 * a2], axis=0)


def _merge(ya, yb, yc, gates, x, wb, wo, gate1, norm2_g, shift2, scale2, wr, br):
    b, l, d = x.shape
    tm = min(TM_MERGE, l)
    nt = l // tm
    tok = lambda bi, i: (bi, i, 0)
    per_b = lambda bi, i: (bi, 0, 0)
    const2 = lambda bi, i: (0, 0)
    in_specs = [
        pl.BlockSpec((1, tm, 512), tok), pl.BlockSpec((1, tm, 512), tok), pl.BlockSpec((1, tm, 512), tok),
        pl.BlockSpec((1, tm, 3 * d), tok), pl.BlockSpec((1, tm, d), tok),
        pl.BlockSpec((3, 512, d), lambda bi, i: (0, 0, 0)), pl.BlockSpec((d, d), const2),
        pl.BlockSpec((1, 1, d), per_b), pl.BlockSpec((1, d), const2),
        pl.BlockSpec((1, 1, d), per_b), pl.BlockSpec((1, 1, d), per_b),
        pl.BlockSpec((d, 2 * ROUTER_COLS), const2), pl.BlockSpec((1, ROUTER_COLS), const2),
    ]
    args = [ya, yb, yc, gates, x, wb, wo, gate1, norm2_g.reshape(1, d), shift2, scale2, wr, br]
    return pl.pallas_call(
        _merge_kernel,
        grid=(b, nt),
        in_specs=in_specs,
        out_specs=[
            pl.BlockSpec((1, tm, d), tok),
            pl.BlockSpec((tm * XROWS, LANES_V7X), lambda bi, i: (bi * nt + i, 0)),
            pl.BlockSpec((2, tm), lambda bi, i: (0, bi * nt + i)),
            pl.BlockSpec((2, tm), lambda bi, i: (0, bi * nt + i)),
        ],
        out_shape=[
            jax.ShapeDtypeStruct((b, l, d), F32),
            jax.ShapeDtypeStruct((b * l * XROWS, LANES_V7X), U32),
            jax.ShapeDtypeStruct((2, b * l), I32),
            jax.ShapeDtypeStruct((2, b * l), F32),
        ],
        name="merge_router",
        compiler_params=_cparams(("arbitrary", "arbitrary")),
    )(*args)


def _plan_kernel(re_ref, tri_ref, dest_ref, be_ref, na_ref, cnt_ref, *, ntok, nblk_pad):
    nt = ntok // PLAN_TILE
    ei = lax.broadcasted_iota(I32, (N_EXPERTS, PLAN_TILE), 0)
    cnt_ref[...] = jnp.zeros_like(cnt_ref)

    def rank_tile(i, carry):
        c0 = pl.multiple_of(i * PLAN_TILE, PLAN_TILE)
        e = re_ref[:, pl.ds(c0, PLAN_TILE)]
        hit1 = ei == e[0:1]
        hit2 = ei == e[1:2]
        onehot = jnp.where(hit1 | hit2, 1.0, 0.0)
        before = _dot(onehot.astype(BF16), tri_ref[...]) + cnt_ref[:, 0:1]
        r1 = jnp.where(hit1, before, 0.0).sum(axis=0, keepdims=True)
        r2 = jnp.where(hit2, before, 0.0).sum(axis=0, keepdims=True)
        dest_ref[:, pl.ds(c0, PLAN_TILE)] = jnp.concatenate([r1, r2], axis=0).astype(I32)
        cnt_ref[...] = cnt_ref[...] + onehot.sum(axis=1, keepdims=True)
        return carry

    lax.fori_loop(0, nt, rank_tile, 0)

    cnt = cnt_ref[...]
    padded = jnp.floor((cnt + (MOE_BLK - 1)) * (1.0 / MOE_BLK)) * MOE_BLK
    sub = lax.broadcasted_iota(I32, (N_EXPERTS, LANES_V7X), 0)
    pstart = jnp.zeros_like(padded)
    for e in range(N_EXPERTS - 1):
        pstart = pstart + jnp.where(sub > e, padded[e:e + 1, :], 0.0)
    pend = pstart + padded
    cnt_ref[...] = pstart

    def dest_tile(i, carry):
        c0 = pl.multiple_of(i * PLAN_TILE, PLAN_TILE)
        e = re_ref[:, pl.ds(c0, PLAN_TILE)]
        ps = cnt_ref[:, 0:1]
        s1 = jnp.where(ei == e[0:1], ps, 0.0).sum(axis=0, keepdims=True)
        s2 = jnp.where(ei == e[1:2], ps, 0.0).sum(axis=0, keepdims=True)
        dest_ref[:, pl.ds(c0, PLAN_TILE)] = (dest_ref[:, pl.ds(c0, PLAN_TILE)]
                                              + jnp.concatenate([s1, s2], axis=0).astype(I32))
        return carry

    lax.fori_loop(0, nt, dest_tile, 0)

    first_row = (lax.broadcasted_iota(I32, (N_EXPERTS, nblk_pad), 1) * MOE_BLK).astype(F32)
    owner = jnp.where(pend[:, 0:1] <= first_row, 1.0, 0.0).sum(axis=0, keepdims=True)
    be_ref[...] = jnp.minimum(owner, float(N_EXPERTS - 1)).astype(I32)
    na_ref[...] = (pend[N_EXPERTS - 1:N_EXPERTS, :] * (1.0 / MOE_BLK)).astype(I32)


def _moe_slots(ntok):
    nblk = -(-(2 * ntok + N_EXPERTS * (MOE_BLK - 1)) // MOE_BLK)
    return nblk, -(-nblk // LANES_V7X) * LANES_V7X


def _plan(re, tri):
    ntok = re.shape[1]
    assert ntok % PLAN_TILE == 0
    _, nblk_pad = _moe_slots(ntok)
    return pl.pallas_call(
        functools.partial(_plan_kernel, ntok=ntok, nblk_pad=nblk_pad),
        out_shape=[
            jax.ShapeDtypeStruct((2, ntok), I32),
            jax.ShapeDtypeStruct((1, nblk_pad), I32),
            jax.ShapeDtypeStruct((1, LANES_V7X), I32),
        ],
        scratch_shapes=[pltpu.VMEM((N_EXPERTS, LANES_V7X), F32)],
        name="moe_plan",
        compiler_params=pltpu.CompilerParams(vmem_limit_bytes=VMEM_LIMIT_V7X),
    )(re, tri)


def _dispatch_kernel(dest_ref, hp_ref, xs_in_ref, xs_ref, sem):
    del xs_in_ref
    tm = hp_ref.shape[0] // XROWS

    def row_copy(i, k):
        src = hp_ref.at[pl.ds(pl.multiple_of(i * XROWS, XROWS), XROWS)]
        dst = xs_ref.at[pl.ds(pl.multiple_of(dest_ref[k, i] * XROWS, XROWS), XROWS)]
        return pltpu.make_async_copy(src, dst, sem)

    def issue(i, carry):
        for k in range(2):
            row_copy(i, k).start(priority=k)
        return carry

    lax.fori_loop(0, tm, issue, 0, unroll=8)

    def drain(i, carry):
        for k in range(2):
            row_copy(i, k).wait()
        return carry

    lax.fori_loop(0, tm, drain, 0, unroll=8)


def _dispatch(dest, hp, xs, tok_off):
    ntok, width = hp.shape[0] // XROWS, hp.shape[1]
    tm = min(TM_DISPATCH, ntok)
    assert tok_off % tm == 0 and ntok % tm == 0
    off = tok_off // tm
    return pl.pallas_call(
        _dispatch_kernel,
        grid=(ntok // tm,),
        in_specs=[
            pl.BlockSpec((2, tm), lambda i: (0, off + i), memory_space=pltpu.SMEM),
            pl.BlockSpec((tm * XROWS, width), lambda i: (i, 0)),
            pl.BlockSpec(memory_space=pl.ANY),
        ],
        out_specs=pl.BlockSpec(memory_space=pl.ANY),
        out_shape=jax.ShapeDtypeStruct(xs.shape, U32),
        scratch_shapes=[pltpu.SemaphoreType.DMA(())],
        input_output_aliases={2: 0},
        name="moe_dispatch",
        compiler_params=_cparams(("arbitrary",)),
    )(dest, hp, xs)


def _experts_kernel(be_ref, na_ref, xs_ref, wg_ref, wu_ref, wd_ref, y_ref, wgb_ref, wub_ref, wdb_ref):
    i = pl.program_id(0)
    active = i < na_ref[0]
    new_expert = (i == 0) | (be_ref[i] != be_ref[jnp.maximum(i - 1, 0)])

    @pl.when(active & new_expert)
    def _():
        wgb_ref[...] = wg_ref[0].astype(BF16)
        wub_ref[...] = wu_ref[0].astype(BF16)
        wdb_ref[...] = wd_ref[0].astype(BF16)

    @pl.when(active)
    def _():
        half = D_MODEL // 2
        sub = MOE_BLK // EXPERT_SPLIT
        for part in range(EXPERT_SPLIT):
            xw = _load_row_tiles(xs_ref, (), part * sub, sub, XROWS)
            lo = pltpu.bitcast(xw << 16, F32).astype(BF16)
            hi = pltpu.bitcast(xw & jnp.uint32(0xFFFF0000), F32).astype(BF16)
            g = _dot(lo, wgb_ref[0:half, :]) + _dot(hi, wgb_ref[half:, :])
            u = _dot(lo, wub_ref[0:half, :]) + _dot(hi, wub_ref[half:, :])
            a = (g * _sigmoid(g) * u).astype(BF16)
            _store_row_tiles(y_ref, (), part * sub, _dot(a, wdb_ref[...]))

    @pl.when(pl.program_id(0) >= na_ref[0])
    def _():
        y_ref[...] = jnp.zeros_like(y_ref)


def _experts(be, na, xs, wg, wu, wd, layer):
    nslots, width = xs.shape[0] // XROWS, xs.shape[1]
    nblk = nslots // MOE_BLK
    d = D_MODEL
    last = lambda i, na: jnp.minimum(i, na[0] - 1)
    grid_spec = pltpu.PrefetchScalarGridSpec(
        num_scalar_prefetch=2,
        grid=(nblk,),
        in_specs=[
            pl.BlockSpec((MOE_BLK * XROWS, width), lambda i, be, na: (last(i, na), 0)),
            pl.BlockSpec((None, 1, d, D_EXPERT), lambda i, be, na: (layer, be[last(i, na)], 0, 0)),
            pl.BlockSpec((None, 1, d, D_EXPERT), lambda i, be, na: (layer, be[last(i, na)], 0, 0)),
            pl.BlockSpec((None, 1, D_EXPERT, d), lambda i, be, na: (layer, be[last(i, na)], 0, 0)),
        ],
        out_specs=pl.BlockSpec((MOE_BLK * YROWS, LANES_V7X), lambda i, be, na: (i, 0)),
        scratch_shapes=[pltpu.VMEM((d, D_EXPERT), BF16), pltpu.VMEM((d, D_EXPERT), BF16),
                        pltpu.VMEM((D_EXPERT, d), BF16)],
    )
    return pl.pallas_call(
        _experts_kernel,
        grid_spec=grid_spec,
        out_shape=jax.ShapeDtypeStruct((nslots * YROWS, LANES_V7X), F32),
        name="moe_experts",
        compiler_params=_cparams(("arbitrary",)),
    )(be, na, xs, wg, wu, wd)


def _combine_kernel(*refs, final, nsteps):
    if final:
        dest_ref, next_ref, rw_ref, y_ref, x_ref, g_ref, fg_ref, o_ref, ybuf, sems = refs
    else:
        dest_ref, next_ref, rw_ref, y_ref, x_ref, g_ref, o_ref, ybuf, sems = refs
    tm = x_ref.shape[0]
    step = pl.program_id(0)
    slot = lax.rem(step, 2)

    def row_copy(idx_ref, s, t, k):
        src = y_ref.at[pl.ds(pl.multiple_of(idx_ref[k, t] * YROWS, YROWS), YROWS)]
        dst = ybuf.at[s, k, pl.ds(pl.multiple_of(t * YROWS, YROWS), YROWS)]
        return pltpu.make_async_copy(src, dst, sems.at[s])

    def start_all(idx_ref, s):
        def body(t, carry):
            for k in range(2):
                row_copy(idx_ref, s, t, k).start(priority=k)
            return carry

        lax.fori_loop(0, tm, body, 0, unroll=8)

    @pl.when(step == 0)
    def _():
        start_all(dest_ref, 0)

    @pl.when(step + 1 < nsteps)
    def _():
        start_all(next_ref, 1 - slot)

    def drain(t, carry):
        for k in range(2):
            row_copy(dest_ref, slot, t, k).wait()
        return carry

    lax.fori_loop(0, tm, drain, 0, unroll=8)

    w = rw_ref[...]
    eye = lax.broadcasted_iota(I32, (tm, tm), 0) == lax.broadcasted_iota(I32, (tm, tm), 1)
    w0 = jnp.where(eye, w[0:1, :], 0.0).sum(axis=1, keepdims=True)
    w1 = jnp.where(eye, w[1:2, :], 0.0).sum(axis=1, keepdims=True)
    y0 = _load_row_tiles(ybuf, (slot, 0), 0, tm, YROWS)
    y1 = _load_row_tiles(ybuf, (slot, 1), 0, tm, YROWS)
    xn = x_ref[...] + g_ref[0] * (w0 * y0 + w1 * y1)
    if final:
        ms = jnp.mean(xn * xn, axis=-1, keepdims=True)
        xn = xn * lax.rsqrt(ms + NORM_EPS) * fg_ref[...]
    o_ref[...] = xn


def _combine(dest, rw, y, x, gate, tok_off, final_g):
    b, l, d = x.shape
    tm = min(TM_COMBINE, l)
    assert tok_off % tm == 0 and l % tm == 0
    off = tok_off // tm
    per_b = l // tm
    nsteps = b * l // tm
    final = final_g is not None
    in_specs = [
        pl.BlockSpec((2, tm), lambda i: (0, off + i), memory_space=pltpu.SMEM),
        pl.BlockSpec((2, tm), lambda i: (0, off + jnp.minimum(i + 1, nsteps - 1)), memory_space=pltpu.SMEM),
        pl.BlockSpec((2, tm), lambda i: (0, off + i)),
        pl.BlockSpec(memory_space=pl.ANY),
        pl.BlockSpec((tm, d), lambda i: (i, 0)),
        pl.BlockSpec((1, 1, d), lambda i: (i // per_b, 0, 0)),
    ]
    args = [dest, dest, rw, y, x.reshape(b * l, d), gate]
    if final:
        in_specs.append(pl.BlockSpec((1, d), lambda i: (0, 0)))
        args.append(final_g.reshape(1, d))
    out = pl.pallas_call(
        functools.partial(_combine_kernel, final=final, nsteps=nsteps),
        grid=(nsteps,),
        in_specs=in_specs,
        out_specs=pl.BlockSpec((tm, d), lambda i: (i, 0)),
        out_shape=jax.ShapeDtypeStruct((b * l, d), F32),
        scratch_shapes=[pltpu.VMEM((2, 2, tm * YROWS, LANES_V7X), F32), pltpu.SemaphoreType.DMA((2,))],
        name="moe_combine_final" if final else "moe_combine",
        compiler_params=_cparams(("arbitrary",)),
    )(*args)
    return out.reshape(b, l, d)


def _permute_w_in(w):
    ck0, cv0, gt0 = 2560, 2688, 2816
    dup = lambda c0: [w[:, c0 + g * HEAD_DIM:c0 + (g + 1) * HEAD_DIM] for g in range(SWA_KV_HEADS) for _ in range(2)]
    cols = [w[:, :ck0]] + dup(ck0) + dup(cv0) + [w[:, gt0:]]
    return jnp.concatenate(cols, axis=1).astype(BF16)


def _rope_tables(seq):
    t = jnp.arange(seq, dtype=I32)
    row = (t // GRID_W).astype(F32)
    col = (t % GRID_W).astype(F32)
    inv = ROPE_THETA ** (-jnp.arange(0, ROPE_AXIS_DIM, 2, dtype=F32) / ROPE_AXIS_DIM)
    ang_r = row[:, None] * inv[None, :]
    ang_c = col[:, None] * inv[None, :]
    cos = jnp.concatenate([jnp.cos(ang_r)] * 2 + [jnp.cos(ang_c)] * 2, axis=1)
    sin = jnp.concatenate([-jnp.sin(ang_r), jnp.sin(ang_r), -jnp.sin(ang_c), jnp.sin(ang_c)], axis=1)
    return jnp.tile(cos, (1, 2)), jnp.tile(sin, (1, 2))


def _router_params(w_rg, b_rg, w_re, b_re):
    d = w_rg.shape[0]
    pad = ROUTER_COLS - N_EXPERTS - N_GROUPS
    w = jnp.concatenate([w_re, w_rg, jnp.zeros((d, pad), F32)], axis=1)
    w_hi = w.astype(BF16)
    w_lo = (w - w_hi.astype(F32)).astype(BF16)
    br = jnp.concatenate([b_re, b_rg, jnp.zeros((pad,), F32)]).reshape(1, ROUTER_COLS)
    return jnp.concatenate([w_hi, w_lo], axis=1), br


def _moe(re, hp_parts, wg, wu, wd, layer, tri):
    ntok = re.shape[1]
    nblk, _ = _moe_slots(ntok)
    dest, be, na = _plan(re, tri)
    xs = jnp.zeros((nblk * MOE_BLK * XROWS, LANES_V7X), U32)
    tok_off = 0
    for hp in hp_parts:
        xs = _dispatch(dest, hp, xs, tok_off)
        tok_off += hp.shape[0] // XROWS
    y = _experts(be.reshape(-1), na.reshape(-1)[:1], xs, wg, wu, wd, layer)
    return dest, y


def kernel(x, c, ctx, c_ctx, w_mod, b_mod, norm1_g, norm2_g, w_in, rpb_a, w_pool, pool_scale, sink_c,
           w_branch, w_out, w_router_group, b_router_group, w_router_expert, b_router_expert,
           w_exp_gate, w_exp_up, w_exp_down, final_g):
    b, s, d = x.shape
    lc = ctx.shape[1]
    depth = w_mod.shape[0]
    nz = b * lc

    pad_rows = (-(b + 1)) % 8
    cc = jnp.concatenate([c, c_ctx[None, :], jnp.zeros((pad_rows, d), F32)], axis=0)
    mod = _modvec(cc, w_mod, b_mod)
    cos, sin = _rope_tables(s)
    mask_bias = _swa_mask_bias()
    tri = jnp.asarray(np.triu(np.ones((PLAN_TILE, PLAN_TILE), np.float32), k=1), BF16)

    z = ctx
    for l in range(depth):
        last = l == depth - 1
        mx = [mod[l, :b, j * d:(j + 1) * d].reshape(b, 1, d) for j in range(N_MOD)]
        mz = [jnp.broadcast_to(mod[l, b:b + 1, j * d:(j + 1) * d], (b, d)).reshape(b, 1, d) for j in range(N_MOD)]
        w_perm = _permute_w_in(w_in[l])
        wp = w_pool[l].astype(BF16)
        wb = w_branch[l].astype(BF16)
        wo = w_out[l].astype(BF16)
        wr, br = _router_params(w_router_group[l], b_router_group[l], w_router_expert[l], b_router_expert[l])

        qkv_x, u_x, qkc_x, vc_x, gt_x = _inproj(x, mx[0], mx[1], norm1_g[l], w_perm, cos, sin)
        qkv_z, u_z, qkc_z, vc_z, gt_z = _inproj(z, mz[0], mz[1], norm1_g[l], w_perm, None, None)
        ya = _natten(qkv_x, qkv_z, _natten_bias(rpb_a[l]))
        yb = _pool(u_x, wp, pool_scale[l])
        yc = _swa(qkc_x, vc_x, qkc_z, vc_z, mask_bias, sink_c[l])

        x, hp_x, re, rw = _merge(ya, yb, yc, gt_x, x, wb, wo, mx[2], norm2_g[l], mx[3], mx[4], wr, br)
        hp_parts = [hp_x]
        x_off = 0
        if not last:
            za = _ctx_attn_a(qkv_z)
            zb = _pool(u_z, wp, pool_scale[l])
            zc = _ctx_attn_c(qkc_z, vc_z, sink_c[l])
            z, hp_z, re_z, rw_z = _merge(za, zb, zc, gt_z, z, wb, wo, mz[2], norm2_g[l], mz[3], mz[4], wr, br)
            re = jnp.concatenate([re_z, re], axis=1)
            rw = jnp.concatenate([rw_z, rw], axis=1)
            hp_parts = [hp_z, hp_x]
            x_off = nz

        dest, y = _moe(re, hp_parts, w_exp_gate, w_exp_up, w_exp_down, l, tri)
        if not last:
            z = _combine(dest, rw, y, z, mz[5], 0, None)
        x = _combine(dest, rw, y, x, mx[5], x_off, final_g if last else None)
    return x
```

```python
import functools

import numpy as np
import jax
import jax.numpy as jnp
from jax import lax
from jax.experimental import pallas as pl
from jax.experimental.pallas import tpu as pltpu

F32 = jnp.float32
BF16 = jnp.bfloat16
I32 = jnp.int32
U32 = jnp.uint32

D_MODEL = 1024
GRID_W = 64
HEAD_DIM = 64
ATTN_SCALE = HEAD_DIM ** -0.5
LOG2E = 1.4426950408889634
Q_SCALE = ATTN_SCALE * LOG2E
NA_HEADS = 8
WIN_ROWS = 8
WIN_COLS = 16
POOL_WINDOWS = (2, 4, 8, 16)
POOL_GW = 128
SWA_HEADS = 8
SWA_KV_HEADS = 2
SWA_WINDOW = 128
SWA_BLK = 128
ROPE_THETA = 10000.0
ROPE_AXIS_DIM = HEAD_DIM // 2
N_GROUPS = 4
EXP_PER_GROUP = 8
N_EXPERTS = 32
D_EXPERT = 512
N_MOD = 6
NORM_EPS = 1e-6
NEG_INF = -1e30

LANES_V7X = 128
VMEM_LIMIT_V7X = 56 * 1024 * 1024

COL_AQ, COL_AK, COL_AV, COL_BU = 0, 512, 1024, 1536
COL_CQ, COL_CK, COL_CV, COL_GT = 2048, 2560, 2816, 3072
IN_COLS = 6144

TM_PROJ = 512
TM_MERGE = 512
MERGE_SPLIT = 2
POOL_CHUNK = 512
PLAN_TILE = 512
MOE_BLK = 512
EXPERT_SPLIT = 2
TM_DISPATCH = 512
TM_COMBINE = 256
XROWS = (D_MODEL // 2) // LANES_V7X
YROWS = D_MODEL // LANES_V7X
NAT_ROWS_PER_ITER = 16
SWA_BLKS_PER_ITER = 4


def _cparams(sem, vmem=VMEM_LIMIT_V7X):
    return pltpu.CompilerParams(dimension_semantics=sem, vmem_limit_bytes=vmem)


def _dot(a, b):
    return jnp.dot(a, b, preferred_element_type=F32)


def _store_row_tiles(ref, lead, first_token, value):
    n, width = value.shape
    r = width // LANES_V7X
    for q in range(r):
        idx = lead + (pl.ds(first_token * r + q, n, stride=r), slice(None))
        ref[idx] = value[:, q * LANES_V7X:(q + 1) * LANES_V7X]


def _load_row_tiles(ref, lead, first_token, n, r):
    return jnp.concatenate(
        [ref[lead + (pl.ds(first_token * r + q, n, stride=r), slice(None))] for q in range(r)], axis=1)


def _dot_nt(a, b):
    return lax.dot_general(a, b, (((1,), (1,)), ((), ())), preferred_element_type=F32)


def _modvec_kernel(c_ref, w_ref, b_ref, o_ref):
    c = c_ref[...]
    s = c / (1.0 + jnp.exp(-c))
    o_ref[0] = jnp.dot(s, w_ref[0], preferred_element_type=F32,
                       precision=lax.Precision.HIGHEST) + b_ref[0]


def _modvec(cc, w_mod, b_mod):
    depth, d, n = w_mod.shape
    rows = cc.shape[0]
    tn = 1536
    return pl.pallas_call(
        _modvec_kernel,
        grid=(depth, n // tn),
        in_specs=[
            pl.BlockSpec((rows, d), lambda l, j: (0, 0)),
            pl.BlockSpec((1, d, tn), lambda l, j: (l, 0, j)),
            pl.BlockSpec((1, 1, tn), lambda l, j: (l, 0, j)),
        ],
        out_specs=pl.BlockSpec((1, rows, tn), lambda l, j: (l, 0, j)),
        out_shape=jax.ShapeDtypeStruct((depth, rows, n), F32),
        name="modvec",
        compiler_params=_cparams(("arbitrary", "arbitrary")),
    )(cc, w_mod, b_mod.reshape(depth, 1, n))


def _rope_apply(t, cos, sin, first_half):
    outs = []
    for j in range(t.shape[1] // LANES_V7X):
        tj = t[:, j * LANES_V7X:(j + 1) * LANES_V7X]
        partner = jnp.where(first_half, pltpu.roll(tj, LANES_V7X - 16, 1), pltpu.roll(tj, 16, 1))
        outs.append(tj * cos + partner * sin)
    return outs[0] if len(outs) == 1 else jnp.concatenate(outs, axis=1)


def _inproj_kernel(*refs, rope):
    if rope:
        (x_ref, sh_ref, sc_ref, g_ref, w_ref, cos_ref, sin_ref,
         qkv_ref, u_ref, qkc_ref, vc_ref, gt_ref) = refs
    else:
        (x_ref, sh_ref, sc_ref, g_ref, w_ref,
         qkv_ref, u_ref, qkc_ref, vc_ref, gt_ref) = refs
    x = x_ref[0]
    ms = jnp.mean(x * x, axis=-1, keepdims=True)
    y = x * lax.rsqrt(ms + NORM_EPS) * g_ref[...]
    h = (y * (1.0 + sc_ref[0]) + sh_ref[0]).astype(BF16)

    def proj(a, b):
        return _dot(h, w_ref[:, a:b])

    qkv_ref[0, :, 0:512] = (proj(COL_AQ, COL_AQ + 512) * Q_SCALE).astype(BF16)
    qkv_ref[0, :, 512:1024] = proj(COL_AK, COL_AK + 512).astype(BF16)
    qkv_ref[0, :, 1024:1536] = proj(COL_AV, COL_AV + 512).astype(BF16)
    u_ref[0] = proj(COL_BU, COL_BU + 512)
    cq = proj(COL_CQ, COL_CQ + 512)
    ck = proj(COL_CK, COL_CK + 256)
    if rope:
        lane = lax.broadcasted_iota(I32, (x.shape[0], LANES_V7X), 1)
        first_half = (lane % 32) < 16
        cos = cos_ref[...]
        sin = sin_ref[...]
        cq = _rope_apply(cq, cos, sin, first_half)
        ck = _rope_apply(ck, cos, sin, first_half)
    qkc_ref[0, :, 0:512] = (cq * Q_SCALE).astype(BF16)
    qkc_ref[0, :, 512:768] = ck.astype(BF16)
    vc_ref[0] = proj(COL_CV, COL_CV + 256).astype(BF16)
    for j in range(6):
        gt_ref[0, :, j * 512:(j + 1) * 512] = proj(COL_GT + j * 512, COL_GT + (j + 1) * 512).astype(BF16)


def _inproj(x, shift, scale, gain, w_perm, cos, sin):
    b, l, d = x.shape
    tm = min(TM_PROJ, l)
    rope = cos is not None
    in_specs = [
        pl.BlockSpec((1, tm, d), lambda bi, i: (bi, i, 0)),
        pl.BlockSpec((1, 1, d), lambda bi, i: (bi, 0, 0)),
        pl.BlockSpec((1, 1, d), lambda bi, i: (bi, 0, 0)),
        pl.BlockSpec((1, d), lambda bi, i: (0, 0)),
        pl.BlockSpec((d, IN_COLS), lambda bi, i: (0, 0), pipeline_mode=pl.Buffered(1)),
    ]
    args = [x, shift, scale, gain.reshape(1, d), w_perm]
    if rope:
        in_specs += [pl.BlockSpec((tm, LANES_V7X), lambda bi, i: (i, 0)),
                     pl.BlockSpec((tm, LANES_V7X), lambda bi, i: (i, 0))]
        args += [cos, sin]
    widths = (1536, 512, 768, 256, 3072)
    dtypes = (BF16, F32, BF16, BF16, BF16)
    return pl.pallas_call(
        functools.partial(_inproj_kernel, rope=rope),
        grid=(b, l // tm),
        in_specs=in_specs,
        out_specs=[pl.BlockSpec((1, tm, w), lambda bi, i: (bi, i, 0)) for w in widths],
        out_shape=[jax.ShapeDtypeStruct((b, l, w), dt) for w, dt in zip(widths, dtypes)],
        name="inproj_rope" if rope else "inproj_ctx",
        compiler_params=_cparams(("parallel", "parallel")),
    )(*args)


def _stack_heads(q2):
    n = q2.shape[0]
    row = lax.broadcasted_iota(I32, (2 * n, LANES_V7X), 0)
    lane = lax.broadcasted_iota(I32, (2 * n, LANES_V7X), 1)
    keep = (row < n) == (lane < HEAD_DIM)
    return jnp.where(keep, jnp.concatenate([q2, q2], axis=0), jnp.zeros((), q2.dtype))


def _unstack_heads(o):
    n = o.shape[0] // 2
    lane = lax.broadcasted_iota(I32, (n, LANES_V7X), 1)
    return jnp.where(lane < HEAD_DIM, o[:n], o[n:])


def _lane_blocks(blocks):
    for s in blocks:
        for j in range(s.shape[1] // LANES_V7X):
            yield s[:, j * LANES_V7X:(j + 1) * LANES_V7X]


def _joint_softmax(score_blocks, extra_logit=None):
    col = None
    for c in _lane_blocks(score_blocks):
        col = c if col is None else jnp.maximum(col, c)
    m = col.max(axis=1, keepdims=True)
    if extra_logit is not None:
        m = jnp.maximum(m, extra_logit)
    probs = [jnp.exp2(s - m) for s in score_blocks]
    col = None
    for c in _lane_blocks(probs):
        col = c if col is None else col + c
    denom = col.sum(axis=1, keepdims=True)
    if extra_logit is not None:
        denom = denom + jnp.exp2(extra_logit - m)
    return [p.astype(BF16) for p in probs], denom


def _softmax_pv(score_blocks, value_blocks, extra_logit=None):
    probs, denom = _joint_softmax(score_blocks, extra_logit)
    acc = None
    for p, v in zip(probs, value_blocks):
        pv = _dot(p, v)
        acc = pv if acc is None else acc + pv
    return acc / denom


def _natten_kernel(q_ref, k_ref, v_ref, kz_ref, vz_ref, bias_ref, o_ref, *, rows):
    kz = kz_ref[0]
    vz = vz_ref[0]
    win = WIN_ROWS * GRID_W

    nq = 2 * GRID_W
    unroll = NAT_ROWS_PER_ITER

    def body(it, carry):
        tok = [pl.multiple_of((it * unroll + u) * GRID_W, GRID_W) for u in range(unroll)]
        qs = [_stack_heads(q_ref[0, pl.ds(tok[u], GRID_W), :]) for u in range(unroll)]
        s_c_all = _dot_nt(jnp.concatenate(qs, axis=0), kz)
        p_c, o_w, denom = [], [], []
        for u in range(unroll):
            r = it * unroll + u
            r0 = jnp.clip(r - WIN_ROWS // 2, 0, rows - WIN_ROWS)
            cls = r0 - r + (WIN_ROWS - 1)
            k0 = pl.multiple_of(r0 * GRID_W, GRID_W)
            s_w = _dot_nt(qs[u], k_ref[0, pl.ds(k0, win), :]) + bias_ref[0, cls]
            probs, den = _joint_softmax([s_w, s_c_all[u * nq:(u + 1) * nq]])
            o_w.append(_dot(probs[0], v_ref[0, pl.ds(k0, win), :]))
            p_c.append(probs[1])
            denom.append(den)
        o_c_all = _dot(jnp.concatenate(p_c, axis=0), vz)
        for u in range(unroll):
            o = (o_w[u] + o_c_all[u * nq:(u + 1) * nq]) / denom[u]
            o_ref[0, pl.ds(tok[u], GRID_W), :] = _unstack_heads(o).astype(BF16)
        return carry

    lax.fori_loop(0, rows // unroll, body, 0)


def _natten(qkv_x, qkv_z, bias):
    b, s, _ = qkv_x.shape
    lc = qkv_z.shape[1]
    rows = s // GRID_W
    assert rows >= WIN_ROWS and rows % NAT_ROWS_PER_ITER == 0
    npair = NA_HEADS // 2
    blk = lambda off: pl.BlockSpec((1, s, LANES_V7X), lambda hp, bi, off=off: (bi, 0, off + hp))
    blkz = lambda off: pl.BlockSpec((1, lc, LANES_V7X), lambda hp, bi, off=off: (bi, 0, off + hp))
    return pl.pallas_call(
        functools.partial(_natten_kernel, rows=rows),
        grid=(npair, b),
        in_specs=[blk(0), blk(npair), blk(2 * npair), blkz(npair), blkz(2 * npair),
                  pl.BlockSpec((1, WIN_ROWS, 2 * GRID_W, WIN_ROWS * GRID_W), lambda hp, bi: (hp, 0, 0, 0))],
        out_specs=pl.BlockSpec((1, s, LANES_V7X), lambda hp, bi: (bi, 0, hp)),
        out_shape=jax.ShapeDtypeStruct((b, s, NA_HEADS * HEAD_DIM), BF16),
        name="natten",
        compiler_params=_cparams(("parallel", "parallel")),
    )(qkv_x, qkv_x, qkv_x, qkv_z, qkv_z, bias)


def _natten_bias(rpb):
    h = rpb.shape[0]
    qc = np.arange(GRID_W)
    kc = np.arange(GRID_W)
    wstart = np.clip(qc - WIN_COLS // 2, 0, GRID_W - WIN_COLS)
    col_ok = (kc[None, :] >= wstart[:, None]) & (kc[None, :] < wstart[:, None] + WIN_COLS)
    dcol = np.clip(kc[None, :] - qc[:, None] + WIN_COLS - 1, 0, 2 * WIN_COLS - 2)
    e = jnp.where(col_ok[None, None], rpb[:, :, dcol].astype(F32) * LOG2E, NEG_INF)
    idx = np.arange(WIN_ROWS)[:, None] + np.arange(WIN_ROWS)[None, :]
    bc = e[:, idx]
    bc = bc.transpose(0, 1, 3, 2, 4).reshape(h, WIN_ROWS, GRID_W, WIN_ROWS * GRID_W)
    bc = bc.reshape(h // 2, 2, WIN_ROWS, GRID_W, WIN_ROWS * GRID_W).transpose(0, 2, 1, 3, 4)
    return bc.reshape(h // 2, WIN_ROWS, 2 * GRID_W, WIN_ROWS * GRID_W)


def _swa_kernel(sink_ref, q_ref, k_ref, v_ref, kz_ref, vz_ref, mask_ref, o_ref, *, seq):
    g = pl.program_id(0)
    kz = kz_ref[0]
    vz = vz_ref[0]
    nwin = 3 * SWA_BLK
    group = SWA_HEADS // SWA_KV_HEADS
    nq = 2 * SWA_BLK
    npairs = group // 2
    unroll = SWA_BLKS_PER_ITER
    row = lax.broadcasted_iota(I32, (nq, 1), 0)
    sinks = [jnp.where(row < SWA_BLK, sink_ref[group * g + 2 * jj], sink_ref[group * g + 2 * jj + 1]) * LOG2E
             for jj in range(npairs)]

    def body(it, carry):
        q0s = [pl.multiple_of((it * unroll + u) * SWA_BLK, SWA_BLK) for u in range(unroll)]
        chains = [(u, jj) for u in range(unroll) for jj in range(npairs)]
        qs = [_stack_heads(q_ref[0, pl.ds(q0s[u], SWA_BLK), jj * LANES_V7X:(jj + 1) * LANES_V7X])
              for u, jj in chains]
        s_c_all = _dot_nt(jnp.concatenate(qs, axis=0), kz)
        p_c, o_w, denom = [], [], []
        for c, (u, jj) in enumerate(chains):
            start = pl.multiple_of(jnp.clip(q0s[u] - SWA_BLK, 0, seq - nwin), SWA_BLK)
            var = (q0s[u] - start) // SWA_BLK
            mb = mask_ref[var]
            s_w = _dot_nt(qs[c], k_ref[0, pl.ds(start, nwin), :]) + jnp.concatenate([mb, mb], axis=0)
            probs, den = _joint_softmax([s_w, s_c_all[c * nq:(c + 1) * nq]], extra_logit=sinks[jj])
            o_w.append(_dot(probs[0], v_ref[0, pl.ds(start, nwin), :]))
            p_c.append(probs[1])
            denom.append(den)
        o_c_all = _dot(jnp.concatenate(p_c, axis=0), vz)
        for c, (u, jj) in enumerate(chains):
            o = (o_w[c] + o_c_all[c * nq:(c + 1) * nq]) / denom[c]
            o_ref[0, pl.ds(q0s[u], SWA_BLK), jj * LANES_V7X:(jj + 1) * LANES_V7X] = (
                _unstack_heads(o).astype(BF16))
        return carry

    lax.fori_loop(0, seq // (SWA_BLK * unroll), body, 0)


def _swa(qkc_x, vc_x, qkc_z, vc_z, mask_bias, sink):
    b, s, _ = qkc_x.shape
    lc = qkc_z.shape[1]
    assert s % (SWA_BLK * SWA_BLKS_PER_ITER) == 0 and s >= 3 * SWA_BLK
    qw = 2 * LANES_V7X
    return pl.pallas_call(
        functools.partial(_swa_kernel, seq=s),
        grid=(SWA_KV_HEADS, b),
        in_specs=[
            pl.BlockSpec(memory_space=pltpu.SMEM),
            pl.BlockSpec((1, s, qw), lambda g, bi: (bi, 0, g)),
            pl.BlockSpec((1, s, LANES_V7X), lambda g, bi: (bi, 0, 4 + g)),
            pl.BlockSpec((1, s, LANES_V7X), lambda g, bi: (bi, 0, g)),
            pl.BlockSpec((1, lc, LANES_V7X), lambda g, bi: (bi, 0, 4 + g)),
            pl.BlockSpec((1, lc, LANES_V7X), lambda g, bi: (bi, 0, g)),
            pl.BlockSpec((3, SWA_BLK, 3 * SWA_BLK), lambda g, bi: (0, 0, 0)),
        ],
        out_specs=pl.BlockSpec((1, s, qw), lambda g, bi: (bi, 0, g)),
        out_shape=jax.ShapeDtypeStruct((b, s, SWA_HEADS * HEAD_DIM), BF16),
        name="swa",
        compiler_params=_cparams(("parallel", "parallel")),
    )(sink, qkc_x, qkc_x, vc_x, qkc_z, vc_z, mask_bias)


def _swa_mask_bias():
    i = np.arange(SWA_BLK)[:, None]
    j = np.arange(3 * SWA_BLK)[None, :]
    out = np.zeros((3, SWA_BLK, 3 * SWA_BLK), np.float32)
    for v in range(3):
        rel = j - v * SWA_BLK - i
        out[v] = np.where(np.abs(rel) <= SWA_WINDOW, 0.0, NEG_INF)
    return jnp.asarray(out)


def _ctx_attn_kernel(*refs, ngroups, use_sink):
    if use_sink:
        sink_ref, q_ref, k_ref, v_ref, o_ref = refs
    else:
        q_ref, k_ref, v_ref, o_ref = refs
    g = pl.program_id(0)
    k = k_ref[0]
    v = v_ref[0]
    n = q_ref.shape[1]
    row = lax.broadcasted_iota(I32, (2 * n, 1), 0)
    for jj in range(ngroups):
        qs = _stack_heads(q_ref[0, :, jj * LANES_V7X:(jj + 1) * LANES_V7X])
        s = _dot_nt(qs, k)
        sink = None
        if use_sink:
            base = 2 * ngroups * g + 2 * jj
            sink = jnp.where(row < n, sink_ref[base], sink_ref[base + 1]) * LOG2E
        o = _softmax_pv([s], [v], extra_logit=sink)
        o_ref[0, :, jj * LANES_V7X:(jj + 1) * LANES_V7X] = _unstack_heads(o).astype(BF16)


def _ctx_attn_a(qkv_z):
    b, lc, _ = qkv_z.shape
    npair = NA_HEADS // 2
    blk = lambda off: pl.BlockSpec((1, lc, LANES_V7X), lambda hp, bi, off=off: (bi, 0, off + hp))
    return pl.pallas_call(
        functools.partial(_ctx_attn_kernel, ngroups=1, use_sink=False),
        grid=(npair, b),
        in_specs=[blk(0), blk(npair), blk(2 * npair)],
        out_specs=pl.BlockSpec((1, lc, LANES_V7X), lambda hp, bi: (bi, 0, hp)),
        out_shape=jax.ShapeDtypeStruct((b, lc, NA_HEADS * HEAD_DIM), BF16),
        name="ctx_attn_a",
        compiler_params=_cparams(("parallel", "parallel")),
    )(qkv_z, qkv_z, qkv_z)


def _ctx_attn_c(qkc_z, vc_z, sink):
    b, lc, _ = qkc_z.shape
    qw = 2 * LANES_V7X
    return pl.pallas_call(
        functools.partial(_ctx_attn_kernel, ngroups=2, use_sink=True),
        grid=(SWA_KV_HEADS, b),
        in_specs=[
            pl.BlockSpec(memory_space=pltpu.SMEM),
            pl.BlockSpec((1, lc, qw), lambda g, bi: (bi, 0, g)),
            pl.BlockSpec((1, lc, LANES_V7X), lambda g, bi: (bi, 0, 4 + g)),
            pl.BlockSpec((1, lc, LANES_V7X), lambda g, bi: (bi, 0, g)),
        ],
        out_specs=pl.BlockSpec((1, lc, qw), lambda g, bi: (bi, 0, g)),
        out_shape=jax.ShapeDtypeStruct((b, lc, SWA_HEADS * HEAD_DIM), BF16),
        name="ctx_attn_c",
        compiler_params=_cparams(("parallel", "parallel")),
    )(sink, qkc_z, qkc_z, vc_z)


PAD_ROWS = 8


def _pool_kernel(u_ref, wp_ref, ps_ref, o_ref, pad_ref, *, length, chunk):
    zeros = jnp.zeros((PAD_ROWS, POOL_GW), F32)
    pad_ref[0:PAD_ROWS, :] = zeros
    pad_ref[length + PAD_ROWS:length + 2 * PAD_ROWS, :] = zeros
    for g, w in enumerate(POOL_WINDOWS):
        half = w // 2
        assert half <= PAD_ROWS
        lanes = slice(g * POOL_GW, (g + 1) * POOL_GW)
        pad_ref[PAD_ROWS:length + PAD_ROWS, :] = u_ref[0, :, lanes]
        for c in range(length // chunk):
            c0 = c * chunk
            acc = pad_ref[c0 + PAD_ROWS - half:c0 + PAD_ROWS - half + chunk, :]
            for j in range(-half + 1, half):
                acc = acc + pad_ref[c0 + PAD_ROWS + j:c0 + PAD_ROWS + j + chunk, :]
            t = c0 + lax.broadcasted_iota(I32, (chunk, 1), 0)
            cnt = (jnp.minimum(t + half, length) - jnp.maximum(t - half, 0)).astype(F32)
            centre = pad_ref[c0 + PAD_ROWS:c0 + PAD_ROWS + chunk, :]
            d = acc / cnt - centre
            y = _dot(d.astype(BF16), wp_ref[g]) * ps_ref[:, lanes]
            o_ref[0, c0:c0 + chunk, lanes] = y.astype(BF16)


def _pool(u, w_pool, pool_scale):
    b, l, width = u.shape
    chunk = min(POOL_CHUNK, l)
    return pl.pallas_call(
        functools.partial(_pool_kernel, length=l, chunk=chunk),
        grid=(b,),
        in_specs=[
            pl.BlockSpec((1, l, width), lambda bi: (bi, 0, 0)),
            pl.BlockSpec(w_pool.shape, lambda bi: (0, 0, 0)),
            pl.BlockSpec((1, width), lambda bi: (0, 0)),
        ],
        out_specs=pl.BlockSpec((1, l, width), lambda bi: (bi, 0, 0)),
        out_shape=jax.ShapeDtypeStruct((b, l, width), BF16),
        scratch_shapes=[pltpu.VMEM((l + 2 * PAD_ROWS, POOL_GW), F32)],
        name="pool",
        compiler_params=_cparams(("parallel",)),
    )(u, w_pool, pool_scale.reshape(1, width))


ROUTER_COLS = LANES_V7X


def _sigmoid(x):
    return 0.5 * jnp.tanh(0.5 * x) + 0.5


def _merge_kernel(ya_ref, yb_ref, yc_ref, gt_ref, x_ref, wb_ref, wo_ref, g1_ref, n2_ref, sh_ref, sc_ref,
                  wr_ref, br_ref, xo_ref, hp_ref, re_ref, rw_ref):
    tm = x_ref.shape[1]
    sub = tm // MERGE_SPLIT
    for part in range(MERGE_SPLIT):
        _merge_rows(slice(part * sub, (part + 1) * sub), ya_ref, yb_ref, yc_ref, gt_ref, x_ref, wb_ref, wo_ref,
                    g1_ref, n2_ref, sh_ref, sc_ref, wr_ref, br_ref, xo_ref, hp_ref, re_ref, rw_ref)


def _merge_rows(rows, ya_ref, yb_ref, yc_ref, gt_ref, x_ref, wb_ref, wo_ref, g1_ref, n2_ref, sh_ref, sc_ref,
                wr_ref, br_ref, xo_ref, hp_ref, re_ref, rw_ref):
    d = D_MODEL
    m = None
    for j, y_ref in enumerate((ya_ref, yb_ref, yc_ref)):
        gate = _sigmoid(gt_ref[0, rows, j * d:(j + 1) * d].astype(F32))
        term = gate * _dot(y_ref[0, rows, :], wb_ref[j])
        m = term if m is None else m + term
    out = _dot(m.astype(BF16), wo_ref[...])
    xn = x_ref[0, rows, :] + g1_ref[0] * out
    xo_ref[0, rows, :] = xn

    ms = jnp.mean(xn * xn, axis=-1, keepdims=True)
    h = xn * lax.rsqrt(ms + NORM_EPS) * n2_ref[...]
    h = h * (1.0 + sc_ref[0]) + sh_ref[0]

    half = d // 2
    h_hi = h.astype(BF16)
    h_hi32 = h_hi.astype(F32)
    lo = pltpu.bitcast(h_hi32[:, :half], U32)
    hi = pltpu.bitcast(h_hi32[:, half:], U32)
    _store_row_tiles(hp_ref, (), rows.start, (lo >> 16) | (hi & jnp.uint32(0xFFFF0000)))

    h_lo = (h - h_hi32).astype(BF16)
    cross = _dot(h_hi, wr_ref[...])
    logits = (cross[:, :ROUTER_COLS] + cross[:, ROUTER_COLS:] + _dot(h_lo, wr_ref[:, :ROUTER_COLS])
              + br_ref[...])
    logits = logits.T
    tm = logits.shape[1]
    le = logits[0:N_EXPERTS]
    lg = logits[N_EXPERTS:N_EXPERTS + N_GROUPS]
    gi = lax.broadcasted_iota(I32, (N_GROUPS, tm), 0).astype(F32)
    lg_max = lg.max(axis=0, keepdims=True)
    g_top = jnp.where(lg == lg_max, gi, float(N_GROUPS)).min(axis=0, keepdims=True)
    p_grp = 1.0 / jnp.exp(lg - lg_max).sum(axis=0, keepdims=True)
    ei_int = lax.broadcasted_iota(I32, (N_EXPERTS, tm), 0)
    ei = ei_int.astype(F32)
    eg = lax.shift_right_logical(ei_int, 3).astype(F32)
    lm = jnp.where(eg == g_top, le, NEG_INF)
    m1 = lm.max(axis=0, keepdims=True)
    i1 = jnp.where(lm == m1, ei, float(N_EXPERTS)).min(axis=0, keepdims=True)
    lm2 = jnp.where(ei == i1, NEG_INF, lm)
    m2 = lm2.max(axis=0, keepdims=True)
    i2 = jnp.where(lm2 == m2, ei, float(N_EXPERTS)).min(axis=0, keepdims=True)
    a2 = jnp.exp(m2 - m1)
    w1 = p_grp / (1.0 + a2)
    re_ref[:, rows] = jnp.concatenate([i1, i2], axis=0).astype(I32)
    rw_ref[:, rows] = jnp.concatenate([w1, w1 * a2], axis=0)


def _merge(ya, yb, yc, gates, x, wb, wo, gate1, norm2_g, shift2, scale2, wr, br):
    b, l, d = x.shape
    tm = min(TM_MERGE, l)
    nt = l // tm
    tok = lambda bi, i: (bi, i, 0)
    per_b = lambda bi, i: (bi, 0, 0)
    const2 = lambda bi, i: (0, 0)
    in_specs = [
        pl.BlockSpec((1, tm, 512), tok), pl.BlockSpec((1, tm, 512), tok), pl.BlockSpec((1, tm, 512), tok),
        pl.BlockSpec((1, tm, 3 * d), tok), pl.BlockSpec((1, tm, d), tok),
        pl.BlockSpec((3, 512, d), lambda bi, i: (0, 0, 0)), pl.BlockSpec((d, d), const2),
        pl.BlockSpec((1, 1, d), per_b), pl.BlockSpec((1, d), const2),
        pl.BlockSpec((1, 1, d), per_b), pl.BlockSpec((1, 1, d), per_b),
        pl.BlockSpec((d, 2 * ROUTER_COLS), const2), pl.BlockSpec((1, ROUTER_COLS), const2),
    ]
    args = [ya, yb, yc, gates, x, wb, wo, gate1, norm2_g.reshape(1, d), shift2, scale2, wr, br]
    return pl.pallas_call(
        _merge_kernel,
        grid=(b, nt),
        in_specs=in_specs,
        out_specs=[
            pl.BlockSpec((1, tm, d), tok),
            pl.BlockSpec((tm * XROWS, LANES_V7X), lambda bi, i: (bi * nt + i, 0)),
            pl.BlockSpec((2, tm), lambda bi, i: (0, bi * nt + i)),
            pl.BlockSpec((2, tm), lambda bi, i: (0, bi * nt + i)),
        ],
        out_shape=[
            jax.ShapeDtypeStruct((b, l, d), F32),
            jax.ShapeDtypeStruct((b * l * XROWS, LANES_V7X), U32),
            jax.ShapeDtypeStruct((2, b * l), I32),
            jax.ShapeDtypeStruct((2, b * l), F32),
        ],
        name="merge_router",
        compiler_params=_cparams(("arbitrary", "arbitrary")),
    )(*args)


def _plan_kernel(re_ref, tri_ref, dest_ref, be_ref, na_ref, cnt_ref, *, ntok, nblk_pad):
    nt = ntok // PLAN_TILE
    ei = lax.broadcasted_iota(I32, (N_EXPERTS, PLAN_TILE), 0)
    cnt_ref[...] = jnp.zeros_like(cnt_ref)

    def rank_tile(i, carry):
        c0 = pl.multiple_of(i * PLAN_TILE, PLAN_TILE)
        e = re_ref[:, pl.ds(c0, PLAN_TILE)]
        hit1 = ei == e[0:1]
        hit2 = ei == e[1:2]
        onehot = jnp.where(hit1 | hit2, 1.0, 0.0)
        before = _dot(onehot.astype(BF16), tri_ref[...]) + cnt_ref[:, 0:1]
        r1 = jnp.where(hit1, before, 0.0).sum(axis=0, keepdims=True)
        r2 = jnp.where(hit2, before, 0.0).sum(axis=0, keepdims=True)
        dest_ref[:, pl.ds(c0, PLAN_TILE)] = jnp.concatenate([r1, r2], axis=0).astype(I32)
        cnt_ref[...] = cnt_ref[...] + onehot.sum(axis=1, keepdims=True)
        return carry

    lax.fori_loop(0, nt, rank_tile, 0)

    cnt = cnt_ref[...]
    padded = jnp.floor((cnt + (MOE_BLK - 1)) * (1.0 / MOE_BLK)) * MOE_BLK
    sub = lax.broadcasted_iota(I32, (N_EXPERTS, LANES_V7X), 0)
    pstart = jnp.zeros_like(padded)
    for e in range(N_EXPERTS - 1):
        pstart = pstart + jnp.where(sub > e, padded[e:e + 1, :], 0.0)
    pend = pstart + padded
    cnt_ref[...] = pstart

    def dest_tile(i, carry):
        c0 = pl.multiple_of(i * PLAN_TILE, PLAN_TILE)
        e = re_ref[:, pl.ds(c0, PLAN_TILE)]
        ps = cnt_ref[:, 0:1]
        s1 = jnp.where(ei == e[0:1], ps, 0.0).sum(axis=0, keepdims=True)
        s2 = jnp.where(ei == e[1:2], ps, 0.0).sum(axis=0, keepdims=True)
        dest_ref[:, pl.ds(c0, PLAN_TILE)] = (dest_ref[:, pl.ds(c0, PLAN_TILE)]
                                              + jnp.concatenate([s1, s2], axis=0).astype(I32))
        return carry

    lax.fori_loop(0, nt, dest_tile, 0)

    first_row = (lax.broadcasted_iota(I32, (N_EXPERTS, nblk_pad), 1) * MOE_BLK).astype(F32)
    owner = jnp.where(pend[:, 0:1] <= first_row, 1.0, 0.0).sum(axis=0, keepdims=True)
    be_ref[...] = jnp.minimum(owner, float(N_EXPERTS - 1)).astype(I32)
    na_ref[...] = (pend[N_EXPERTS - 1:N_EXPERTS, :] * (1.0 / MOE_BLK)).astype(I32)


def _moe_slots(ntok):
    nblk = -(-(2 * ntok + N_EXPERTS * (MOE_BLK - 1)) // MOE_BLK)
    return nblk, -(-nblk // LANES_V7X) * LANES_V7X


def _plan(re, tri):
    ntok = re.shape[1]
    assert ntok % PLAN_TILE == 0
    _, nblk_pad = _moe_slots(ntok)
    return pl.pallas_call(
        functools.partial(_plan_kernel, ntok=ntok, nblk_pad=nblk_pad),
        out_shape=[
            jax.ShapeDtypeStruct((2, ntok), I32),
            jax.ShapeDtypeStruct((1, nblk_pad), I32),
            jax.ShapeDtypeStruct((1, LANES_V7X), I32),
        ],
        scratch_shapes=[pltpu.VMEM((N_EXPERTS, LANES_V7X), F32)],
        name="moe_plan",
        compiler_params=pltpu.CompilerParams(vmem_limit_bytes=VMEM_LIMIT_V7X),
    )(re, tri)


def _dispatch_kernel(dest_ref, hp_ref, xs_in_ref, xs_ref, sem):
    del xs_in_ref
    tm = hp_ref.shape[0] // XROWS

    def row_copy(i, k):
        src = hp_ref.at[pl.ds(pl.multiple_of(i * XROWS, XROWS), XROWS)]
        dst = xs_ref.at[pl.ds(pl.multiple_of(dest_ref[k, i] * XROWS, XROWS), XROWS)]
        return pltpu.make_async_copy(src, dst, sem)

    def issue(i, carry):
        for k in range(2):
            row_copy(i, k).start(priority=k)
        return carry

    lax.fori_loop(0, tm, issue, 0, unroll=8)

    def drain(i, carry):
        for k in range(2):
            row_copy(i, k).wait()
        return carry

    lax.fori_loop(0, tm, drain, 0, unroll=8)


def _dispatch(dest, hp, xs, tok_off):
    ntok, width = hp.shape[0] // XROWS, hp.shape[1]
    tm = min(TM_DISPATCH, ntok)
    assert tok_off % tm == 0 and ntok % tm == 0
    off = tok_off // tm
    return pl.pallas_call(
        _dispatch_kernel,
        grid=(ntok // tm,),
        in_specs=[
            pl.BlockSpec((2, tm), lambda i: (0, off + i), memory_space=pltpu.SMEM),
            pl.BlockSpec((tm * XROWS, width), lambda i: (i, 0)),
            pl.BlockSpec(memory_space=pl.ANY),
        ],
        out_specs=pl.BlockSpec(memory_space=pl.ANY),
        out_shape=jax.ShapeDtypeStruct(xs.shape, U32),
        scratch_shapes=[pltpu.SemaphoreType.DMA(())],
        input_output_aliases={2: 0},
        name="moe_dispatch",
        compiler_params=_cparams(("arbitrary",)),
    )(dest, hp, xs)


def _experts_kernel(be_ref, na_ref, xs_ref, wg_ref, wu_ref, wd_ref, y_ref, wgb_ref, wub_ref, wdb_ref):
    i = pl.program_id(0)
    active = i < na_ref[0]
    new_expert = (i == 0) | (be_ref[i] != be_ref[jnp.maximum(i - 1, 0)])

    @pl.when(active & new_expert)
    def _():
        wgb_ref[...] = wg_ref[0].astype(BF16)
        wub_ref[...] = wu_ref[0].astype(BF16)
        wdb_ref[...] = wd_ref[0].astype(BF16)

    @pl.when(active)
    def _():
        half = D_MODEL // 2
        sub = MOE_BLK // EXPERT_SPLIT
        for part in range(EXPERT_SPLIT):
            xw = _load_row_tiles(xs_ref, (), part * sub, sub, XROWS)
            lo = pltpu.bitcast(xw << 16, F32).astype(BF16)
            hi = pltpu.bitcast(xw & jnp.uint32(0xFFFF0000), F32).astype(BF16)
            g = _dot(lo, wgb_ref[0:half, :]) + _dot(hi, wgb_ref[half:, :])
            u = _dot(lo, wub_ref[0:half, :]) + _dot(hi, wub_ref[half:, :])
            a = (g * _sigmoid(g) * u).astype(BF16)
            _store_row_tiles(y_ref, (), part * sub, _dot(a, wdb_ref[...]))

    @pl.when(pl.program_id(0) >= na_ref[0])
    def _():
        y_ref[...] = jnp.zeros_like(y_ref)


def _experts(be, na, xs, wg, wu, wd, layer):
    nslots, width = xs.shape[0] // XROWS, xs.shape[1]
    nblk = nslots // MOE_BLK
    d = D_MODEL
    last = lambda i, na: jnp.minimum(i, na[0] - 1)
    grid_spec = pltpu.PrefetchScalarGridSpec(
        num_scalar_prefetch=2,
        grid=(nblk,),
        in_specs=[
            pl.BlockSpec((MOE_BLK * XROWS, width), lambda i, be, na: (last(i, na), 0)),
            pl.BlockSpec((None, 1, d, D_EXPERT), lambda i, be, na: (layer, be[last(i, na)], 0, 0)),
            pl.BlockSpec((None, 1, d, D_EXPERT), lambda i, be, na: (layer, be[last(i, na)], 0, 0)),
            pl.BlockSpec((None, 1, D_EXPERT, d), lambda i, be, na: (layer, be[last(i, na)], 0, 0)),
        ],
        out_specs=pl.BlockSpec((MOE_BLK * YROWS, LANES_V7X), lambda i, be, na: (i, 0)),
        scratch_shapes=[pltpu.VMEM((d, D_EXPERT), BF16), pltpu.VMEM((d, D_EXPERT), BF16),
                        pltpu.VMEM((D_EXPERT, d), BF16)],
    )
    return pl.pallas_call(
        _experts_kernel,
        grid_spec=grid_spec,
        out_shape=jax.ShapeDtypeStruct((nslots * YROWS, LANES_V7X), F32),
        name="moe_experts",
        compiler_params=_cparams(("arbitrary",)),
    )(be, na, xs, wg, wu, wd)


def _combine_kernel(*refs, final, nsteps):
    if final:
        dest_ref, next_ref, rw_ref, y_ref, x_ref, g_ref, fg_ref, o_ref, ybuf, sems = refs
    else:
        dest_ref, next_ref, rw_ref, y_ref, x_ref, g_ref, o_ref, ybuf, sems = refs
    tm = x_ref.shape[0]
    step = pl.program_id(0)
    slot = lax.rem(step, 2)

    def row_copy(idx_ref, s, t, k):
        src = y_ref.at[pl.ds(pl.multiple_of(idx_ref[k, t] * YROWS, YROWS), YROWS)]
        dst = ybuf.at[s, k, pl.ds(pl.multiple_of(t * YROWS, YROWS), YROWS)]
        return pltpu.make_async_copy(src, dst, sems.at[s])

    def start_all(idx_ref, s):
        def body(t, carry):
            for k in range(2):
                row_copy(idx_ref, s, t, k).start(priority=k)
            return carry

        lax.fori_loop(0, tm, body, 0, unroll=8)

    @pl.when(step == 0)
    def _():
        start_all(dest_ref, 0)

    @pl.when(step + 1 < nsteps)
    def _():
        start_all(next_ref, 1 - slot)

    def drain(t, carry):
        for k in range(2):
            row_copy(dest_ref, slot, t, k).wait()
        return carry

    lax.fori_loop(0, tm, drain, 0, unroll=8)

    w = rw_ref[...]
    eye = lax.broadcasted_iota(I32, (tm, tm), 0) == lax.broadcasted_iota(I32, (tm, tm), 1)
    w0 = jnp.where(eye, w[0:1, :], 0.0).sum(axis=1, keepdims=True)
    w1 = jnp.where(eye, w[1:2, :], 0.0).sum(axis=1, keepdims=True)
    y0 = _load_row_tiles(ybuf, (slot, 0), 0, tm, YROWS)
    y1 = _load_row_tiles(ybuf, (slot, 1), 0, tm, YROWS)
    xn = x_ref[...] + g_ref[0] * (w0 * y0 + w1 * y1)
    if final:
        ms = jnp.mean(xn * xn, axis=-1, keepdims=True)
        xn = xn * lax.rsqrt(ms + NORM_EPS) * fg_ref[...]
    o_ref[...] = xn


def _combine(dest, rw, y, x, gate, tok_off, final_g):
    b, l, d = x.shape
    tm = min(TM_COMBINE, l)
    assert tok_off % tm == 0 and l % tm == 0
    off = tok_off // tm
    per_b = l // tm
    nsteps = b * l // tm
    final = final_g is not None
    in_specs = [
        pl.BlockSpec((2, tm), lambda i: (0, off + i), memory_space=pltpu.SMEM),
        pl.BlockSpec((2, tm), lambda i: (0, off + jnp.minimum(i + 1, nsteps - 1)), memory_space=pltpu.SMEM),
        pl.BlockSpec((2, tm), lambda i: (0, off + i)),
        pl.BlockSpec(memory_space=pl.ANY),
        pl.BlockSpec((tm, d), lambda i: (i, 0)),
        pl.BlockSpec((1, 1, d), lambda i: (i // per_b, 0, 0)),
    ]
    args = [dest, dest, rw, y, x.reshape(b * l, d), gate]
    if final:
        in_specs.append(pl.BlockSpec((1, d), lambda i: (0, 0)))
        args.append(final_g.reshape(1, d))
    out = pl.pallas_call(
        functools.partial(_combine_kernel, final=final, nsteps=nsteps),
        grid=(nsteps,),
        in_specs=in_specs,
        out_specs=pl.BlockSpec((tm, d), lambda i: (i, 0)),
        out_shape=jax.ShapeDtypeStruct((b * l, d), F32),
        scratch_shapes=[pltpu.VMEM((2, 2, tm * YROWS, LANES_V7X), F32), pltpu.SemaphoreType.DMA((2,))],
        name="moe_combine_final" if final else "moe_combine",
        compiler_params=_cparams(("arbitrary",)),
    )(*args)
    return out.reshape(b, l, d)


def _permute_w_in(w):
    ck0, cv0, gt0 = 2560, 2688, 2816
    dup = lambda c0: [w[:, c0 + g * HEAD_DIM:c0 + (g + 1) * HEAD_DIM] for g in range(SWA_KV_HEADS) for _ in range(2)]
    cols = [w[:, :ck0]] + dup(ck0) + dup(cv0) + [w[:, gt0:]]
    return jnp.concatenate(cols, axis=1).astype(BF16)


def _rope_tables(seq):
    t = jnp.arange(seq, dtype=I32)
    row = (t // GRID_W).astype(F32)
    col = (t % GRID_W).astype(F32)
    inv = ROPE_THETA ** (-jnp.arange(0, ROPE_AXIS_DIM, 2, dtype=F32) / ROPE_AXIS_DIM)
    ang_r = row[:, None] * inv[None, :]
    ang_c = col[:, None] * inv[None, :]
    cos = jnp.concatenate([jnp.cos(ang_r)] * 2 + [jnp.cos(ang_c)] * 2, axis=1)
    sin = jnp.concatenate([-jnp.sin(ang_r), jnp.sin(ang_r), -jnp.sin(ang_c), jnp.sin(ang_c)], axis=1)
    return jnp.tile(cos, (1, 2)), jnp.tile(sin, (1, 2))


def _router_params(w_rg, b_rg, w_re, b_re):
    d = w_rg.shape[0]
    pad = ROUTER_COLS - N_EXPERTS - N_GROUPS
    w = jnp.concatenate([w_re, w_rg, jnp.zeros((d, pad), F32)], axis=1)
    w_hi = w.astype(BF16)
    w_lo = (w - w_hi.astype(F32)).astype(BF16)
    br = jnp.concatenate([b_re, b_rg, jnp.zeros((pad,), F32)]).reshape(1, ROUTER_COLS)
    return jnp.concatenate([w_hi, w_lo], axis=1), br


def _moe(re, hp_parts, wg, wu, wd, layer, tri):
    ntok = re.shape[1]
    nblk, _ = _moe_slots(ntok)
    dest, be, na = _plan(re, tri)
    xs = jnp.zeros((nblk * MOE_BLK * XROWS, LANES_V7X), U32)
    tok_off = 0
    for hp in hp_parts:
        xs = _dispatch(dest, hp, xs, tok_off)
        tok_off += hp.shape[0] // XROWS
    y = _experts(be.reshape(-1), na.reshape(-1)[:1], xs, wg, wu, wd, layer)
    return dest, y


def kernel(x, c, ctx, c_ctx, w_mod, b_mod, norm1_g, norm2_g, w_in, rpb_a, w_pool, pool_scale, sink_c,
           w_branch, w_out, w_router_group, b_router_group, w_router_expert, b_router_expert,
           w_exp_gate, w_exp_up, w_exp_down, final_g):
    b, s, d = x.shape
    lc = ctx.shape[1]
    depth = w_mod.shape[0]
    nz = b * lc

    pad_rows = (-(b + 1)) % 8
    cc = jnp.concatenate([c, c_ctx[None, :], jnp.zeros((pad_rows, d), F32)], axis=0)
    mod = _modvec(cc, w_mod, b_mod)
    cos, sin = _rope_tables(s)
    mask_bias = _swa_mask_bias()
    tri = jnp.asarray(np.triu(np.ones((PLAN_TILE, PLAN_TILE), np.float32), k=1), BF16)

    z = ctx
    for l in range(depth):
        last = l == depth - 1
        mx = [mod[l, :b, j * d:(j + 1) * d].reshape(b, 1, d) for j in range(N_MOD)]
        mz = [jnp.broadcast_to(mod[l, b:b + 1, j * d:(j + 1) * d], (b, d)).reshape(b, 1, d) for j in range(N_MOD)]
        w_perm = _permute_w_in(w_in[l])
        wp = w_pool[l].astype(BF16)
        wb = w_branch[l].astype(BF16)
        wo = w_out[l].astype(BF16)
        wr, br = _router_params(w_router_group[l], b_router_group[l], w_router_expert[l], b_router_expert[l])

        qkv_x, u_x, qkc_x, vc_x, gt_x = _inproj(x, mx[0], mx[1], norm1_g[l], w_perm, cos, sin)
        qkv_z, u_z, qkc_z, vc_z, gt_z = _inproj(z, mz[0], mz[1], norm1_g[l], w_perm, None, None)
        ya = _natten(qkv_x, qkv_z, _natten_bias(rpb_a[l]))
        yb = _pool(u_x, wp, pool_scale[l])
        yc = _swa(qkc_x, vc_x, qkc_z, vc_z, mask_bias, sink_c[l])

        x, hp_x, re, rw = _merge(ya, yb, yc, gt_x, x, wb, wo, mx[2], norm2_g[l], mx[3], mx[4], wr, br)
        hp_parts = [hp_x]
        x_off = 0
        if not last:
            za = _ctx_attn_a(qkv_z)
            zb = _pool(u_z, wp, pool_scale[l])
            zc = _ctx_attn_c(qkc_z, vc_z, sink_c[l])
            z, hp_z, re_z, rw_z = _merge(za, zb, zc, gt_z, z, wb, wo, mz[2], norm2_g[l], mz[3], mz[4], wr, br)
            re = jnp.concatenate([re_z, re], axis=1)
            rw = jnp.concatenate([rw_z, rw], axis=1)
            hp_parts = [hp_z, hp_x]
            x_off = nz

        dest, y = _moe(re, hp_parts, w_exp_gate, w_exp_up, w_exp_down, l, tri)
        if not last:
            z = _combine(dest, rw, y, z, mz[5], 0, None)
        x = _combine(dest, rw, y, x, mx[5], x_off, final_g if last else None)
    return x
```

```python
import functools

import numpy as np
import jax
import jax.numpy as jnp
from jax import lax
from jax.experimental import pallas as pl
from jax.experimental.pallas import tpu as pltpu

F32 = jnp.float32
BF16 = jnp.bfloat16
I32 = jnp.int32
U32 = jnp.uint32

D_MODEL = 1024
GRID_W = 64
HEAD_DIM = 64
ATTN_SCALE = HEAD_DIM ** -0.5
LOG2E = 1.4426950408889634
Q_SCALE = ATTN_SCALE * LOG2E
NA_HEADS = 8
WIN_ROWS = 8
WIN_COLS = 16
POOL_WINDOWS = (2, 4, 8, 16)
POOL_GW = 128
SWA_HEADS = 8
SWA_KV_HEADS = 2
SWA_WINDOW = 128
SWA_BLK = 128
ROPE_THETA = 10000.0
ROPE_AXIS_DIM = HEAD_DIM // 2
N_GROUPS = 4
EXP_PER_GROUP = 8
N_EXPERTS = 32
D_EXPERT = 512
N_MOD = 6
NORM_EPS = 1e-6
NEG_INF = -1e30

LANES_V7X = 128
VMEM_LIMIT_V7X = 56 * 1024 * 1024

COL_AQ, COL_AK, COL_AV, COL_BU = 0, 512, 1024, 1536
COL_CQ, COL_CK, COL_CV, COL_GT = 2048, 2560, 2816, 3072
IN_COLS = 6144

TM_PROJ = 512
TM_MERGE = 512
MERGE_SPLIT = 2
POOL_CHUNK = 512
PLAN_TILE = 512
MOE_BLK = 512
EXPERT_SPLIT = 2
TM_DISPATCH = 512
TM_COMBINE = 256
XROWS = (D_MODEL // 2) // LANES_V7X
YROWS = D_MODEL // LANES_V7X
NAT_ROWS_PER_ITER = 32
SWA_BLKS_PER_ITER = 4


def _cparams(sem, vmem=VMEM_LIMIT_V7X):
    return pltpu.CompilerParams(dimension_semantics=sem, vmem_limit_bytes=vmem)


def _dot(a, b):
    return jnp.dot(a, b, preferred_element_type=F32)


def _store_row_tiles(ref, lead, first_token, value):
    n, width = value.shape
    r = width // LANES_V7X
    for q in range(r):
        idx = lead + (pl.ds(first_token * r + q, n, stride=r), slice(None))
        ref[idx] = value[:, q * LANES_V7X:(q + 1) * LANES_V7X]


def _load_row_tiles(ref, lead, first_token, n, r):
    return jnp.concatenate(
        [ref[lead + (pl.ds(first_token * r + q, n, stride=r), slice(None))] for q in range(r)], axis=1)


def _dot_nt(a, b):
    return lax.dot_general(a, b, (((1,), (1,)), ((), ())), preferred_element_type=F32)


def _modvec_kernel(c_ref, w_ref, b_ref, o_ref):
    c = c_ref[...]
    s = c / (1.0 + jnp.exp(-c))
    o_ref[0] = jnp.dot(s, w_ref[0], preferred_element_type=F32,
                       precision=lax.Precision.HIGHEST) + b_ref[0]


def _modvec(cc, w_mod, b_mod):
    depth, d, n = w_mod.shape
    rows = cc.shape[0]
    tn = 1536
    return pl.pallas_call(
        _modvec_kernel,
        grid=(depth, n // tn),
        in_specs=[
            pl.BlockSpec((rows, d), lambda l, j: (0, 0)),
            pl.BlockSpec((1, d, tn), lambda l, j: (l, 0, j)),
            pl.BlockSpec((1, 1, tn), lambda l, j: (l, 0, j)),
        ],
        out_specs=pl.BlockSpec((1, rows, tn), lambda l, j: (l, 0, j)),
        out_shape=jax.ShapeDtypeStruct((depth, rows, n), F32),
        name="modvec",
        compiler_params=_cparams(("arbitrary", "arbitrary")),
    )(cc, w_mod, b_mod.reshape(depth, 1, n))


def _rope_apply(t, cos, sin, first_half):
    outs = []
    for j in range(t.shape[1] // LANES_V7X):
        tj = t[:, j * LANES_V7X:(j + 1) * LANES_V7X]
        partner = jnp.where(first_half, pltpu.roll(tj, LANES_V7X - 16, 1), pltpu.roll(tj, 16, 1))
        outs.append(tj * cos + partner * sin)
    return outs[0] if len(outs) == 1 else jnp.concatenate(outs, axis=1)


def _inproj_kernel(*refs, rope):
    if rope:
        (x_ref, sh_ref, sc_ref, g_ref, w_ref, cos_ref, sin_ref,
         qkv_ref, u_ref, qkc_ref, vc_ref, gt_ref) = refs
    else:
        (x_ref, sh_ref, sc_ref, g_ref, w_ref,
         qkv_ref, u_ref, qkc_ref, vc_ref, gt_ref) = refs
    x = x_ref[0]
    ms = jnp.mean(x * x, axis=-1, keepdims=True)
    y = x * lax.rsqrt(ms + NORM_EPS) * g_ref[...]
    h = (y * (1.0 + sc_ref[0]) + sh_ref[0]).astype(BF16)

    def proj(a, b):
        return _dot(h, w_ref[:, a:b])

    qkv_ref[0, :, 0:512] = (proj(COL_AQ, COL_AQ + 512) * Q_SCALE).astype(BF16)
    qkv_ref[0, :, 512:1024] = proj(COL_AK, COL_AK + 512).astype(BF16)
    qkv_ref[0, :, 1024:1536] = proj(COL_AV, COL_AV + 512).astype(BF16)
    u_ref[0] = proj(COL_BU, COL_BU + 512)
    cq = proj(COL_CQ, COL_CQ + 512)
    ck = proj(COL_CK, COL_CK + 256)
    if rope:
        lane = lax.broadcasted_iota(I32, (x.shape[0], LANES_V7X), 1)
        first_half = (lane % 32) < 16
        cos = cos_ref[...]
        sin = sin_ref[...]
        cq = _rope_apply(cq, cos, sin, first_half)
        ck = _rope_apply(ck, cos, sin, first_half)
    qkc_ref[0, :, 0:512] = (cq * Q_SCALE).astype(BF16)
    qkc_ref[0, :, 512:768] = ck.astype(BF16)
    vc_ref[0] = proj(COL_CV, COL_CV + 256).astype(BF16)
    for j in range(6):
        gt_ref[0, :, j * 512:(j + 1) * 512] = proj(COL_GT + j * 512, COL_GT + (j + 1) * 512).astype(BF16)


def _inproj(x, shift, scale, gain, w_perm, cos, sin):
    b, l, d = x.shape
    tm = min(TM_PROJ, l)
    rope = cos is not None
    in_specs = [
        pl.BlockSpec((1, tm, d), lambda bi, i: (bi, i, 0)),
        pl.BlockSpec((1, 1, d), lambda bi, i: (bi, 0, 0)),
        pl.BlockSpec((1, 1, d), lambda bi, i: (bi, 0, 0)),
        pl.BlockSpec((1, d), lambda bi, i: (0, 0)),
        pl.BlockSpec((d, IN_COLS), lambda bi, i: (0, 0), pipeline_mode=pl.Buffered(1)),
    ]
    args = [x, shift, scale, gain.reshape(1, d), w_perm]
    if rope:
        in_specs += [pl.BlockSpec((tm, LANES_V7X), lambda bi, i: (i, 0)),
                     pl.BlockSpec((tm, LANES_V7X), lambda bi, i: (i, 0))]
        args += [cos, sin]
    widths = (1536, 512, 768, 256, 3072)
    dtypes = (BF16, F32, BF16, BF16, BF16)
    return pl.pallas_call(
        functools.partial(_inproj_kernel, rope=rope),
        grid=(b, l // tm),
        in_specs=in_specs,
        out_specs=[pl.BlockSpec((1, tm, w), lambda bi, i: (bi, i, 0)) for w in widths],
        out_shape=[jax.ShapeDtypeStruct((b, l, w), dt) for w, dt in zip(widths, dtypes)],
        name="inproj_rope" if rope else "inproj_ctx",
        compiler_params=_cparams(("parallel", "parallel")),
    )(*args)


def _stack_heads(q2):
    n = q2.shape[0]
    row = lax.broadcasted_iota(I32, (2 * n, LANES_V7X), 0)
    lane = lax.broadcasted_iota(I32, (2 * n, LANES_V7X), 1)
    keep = (row < n) == (lane < HEAD_DIM)
    return jnp.where(keep, jnp.concatenate([q2, q2], axis=0), jnp.zeros((), q2.dtype))


def _unstack_heads(o):
    n = o.shape[0] // 2
    lane = lax.broadcasted_iota(I32, (n, LANES_V7X), 1)
    return jnp.where(lane < HEAD_DIM, o[:n], o[n:])


def _lane_blocks(blocks):
    for s in blocks:
        for j in range(s.shape[1] // LANES_V7X):
            yield s[:, j * LANES_V7X:(j + 1) * LANES_V7X]


def _joint_softmax(score_blocks, extra_logit=None):
    col = None
    for c in _lane_blocks(score_blocks):
        col = c if col is None else jnp.maximum(col, c)
    m = col.max(axis=1, keepdims=True)
    if extra_logit is not None:
        m = jnp.maximum(m, extra_logit)
    probs = [jnp.exp2(s - m) for s in score_blocks]
    col = None
    for c in _lane_blocks(probs):
        col = c if col is None else col + c
    denom = col.sum(axis=1, keepdims=True)
    if extra_logit is not None:
        denom = denom + jnp.exp2(extra_logit - m)
    return [p.astype(BF16) for p in probs], denom


def _softmax_pv(score_blocks, value_blocks, extra_logit=None):
    probs, denom = _joint_softmax(score_blocks, extra_logit)
    acc = None
    for p, v in zip(probs, value_blocks):
        pv = _dot(p, v)
        acc = pv if acc is None else acc + pv
    return acc / denom


def _natten_kernel(q_ref, k_ref, v_ref, kz_ref, vz_ref, bias_ref, o_ref, *, rows):
    kz = kz_ref[0]
    vz = vz_ref[0]
    win = WIN_ROWS * GRID_W

    nq = 2 * GRID_W
    unroll = NAT_ROWS_PER_ITER

    def body(it, carry):
        tok = [pl.multiple_of((it * unroll + u) * GRID_W, GRID_W) for u in range(unroll)]
        qs = [_stack_heads(q_ref[0, pl.ds(tok[u], GRID_W), :]) for u in range(unroll)]
        s_c_all = _dot_nt(jnp.concatenate(qs, axis=0), kz)
        p_c, o_w, denom = [], [], []
        for u in range(unroll):
            r = it * unroll + u
            r0 = jnp.clip(r - WIN_ROWS // 2, 0, rows - WIN_ROWS)
            cls = r0 - r + (WIN_ROWS - 1)
            k0 = pl.multiple_of(r0 * GRID_W, GRID_W)
            s_w = _dot_nt(qs[u], k_ref[0, pl.ds(k0, win), :]) + bias_ref[0, cls]
            probs, den = _joint_softmax([s_w, s_c_all[u * nq:(u + 1) * nq]])
            o_w.append(_dot(probs[0], v_ref[0, pl.ds(k0, win), :]))
            p_c.append(probs[1])
            denom.append(den)
        o_c_all = _dot(jnp.concatenate(p_c, axis=0), vz)
        for u in range(unroll):
            o = (o_w[u] + o_c_all[u * nq:(u + 1) * nq]) / denom[u]
            o_ref[0, pl.ds(tok[u], GRID_W), :] = _unstack_heads(o).astype(BF16)
        return carry

    lax.fori_loop(0, rows // unroll, body, 0)


def _natten(qkv_x, qkv_z, bias):
    b, s, _ = qkv_x.shape
    lc = qkv_z.shape[1]
    rows = s // GRID_W
    assert rows >= WIN_ROWS and rows % NAT_ROWS_PER_ITER == 0
    npair = NA_HEADS // 2
    blk = lambda off: pl.BlockSpec((1, s, LANES_V7X), lambda hp, bi, off=off: (bi, 0, off + hp))
    blkz = lambda off: pl.BlockSpec((1, lc, LANES_V7X), lambda hp, bi, off=off: (bi, 0, off + hp))
    return pl.pallas_call(
        functools.partial(_natten_kernel, rows=rows),
        grid=(npair, b),
        in_specs=[blk(0), blk(npair), blk(2 * npair), blkz(npair), blkz(2 * npair),
                  pl.BlockSpec((1, WIN_ROWS, 2 * GRID_W, WIN_ROWS * GRID_W), lambda hp, bi: (hp, 0, 0, 0))],
        out_specs=pl.BlockSpec((1, s, LANES_V7X), lambda hp, bi: (bi, 0, hp)),
        out_shape=jax.ShapeDtypeStruct((b, s, NA_HEADS * HEAD_DIM), BF16),
        name="natten",
        compiler_params=_cparams(("parallel", "parallel")),
    )(qkv_x, qkv_x, qkv_x, qkv_z, qkv_z, bias)


def _natten_bias(rpb):
    h = rpb.shape[0]
    qc = np.arange(GRID_W)
    kc = np.arange(GRID_W)
    wstart = np.clip(qc - WIN_COLS // 2, 0, GRID_W - WIN_COLS)
    col_ok = (kc[None, :] >= wstart[:, None]) & (kc[None, :] < wstart[:, None] + WIN_COLS)
    dcol = np.clip(kc[None, :] - qc[:, None] + WIN_COLS - 1, 0, 2 * WIN_COLS - 2)
    e = jnp.where(col_ok[None, None], rpb[:, :, dcol].astype(F32) * LOG2E, NEG_INF)
    idx = np.arange(WIN_ROWS)[:, None] + np.arange(WIN_ROWS)[None, :]
    bc = e[:, idx]
    bc = bc.transpose(0, 1, 3, 2, 4).reshape(h, WIN_ROWS, GRID_W, WIN_ROWS * GRID_W)
    bc = bc.reshape(h // 2, 2, WIN_ROWS, GRID_W, WIN_ROWS * GRID_W).transpose(0, 2, 1, 3, 4)
    return bc.reshape(h // 2, WIN_ROWS, 2 * GRID_W, WIN_ROWS * GRID_W)


def _swa_kernel(sink_ref, q_ref, k_ref, v_ref, kz_ref, vz_ref, mask_ref, o_ref, *, seq):
    g = pl.program_id(0)
    kz = kz_ref[0]
    vz = vz_ref[0]
    nwin = 3 * SWA_BLK
    group = SWA_HEADS // SWA_KV_HEADS
    nq = 2 * SWA_BLK
    npairs = group // 2
    unroll = SWA_BLKS_PER_ITER
    row = lax.broadcasted_iota(I32, (nq, 1), 0)
    sinks = [jnp.where(row < SWA_BLK, sink_ref[group * g + 2 * jj], sink_ref[group * g + 2 * jj + 1]) * LOG2E
             for jj in range(npairs)]

    def body(it, carry):
        q0s = [pl.multiple_of((it * unroll + u) * SWA_BLK, SWA_BLK) for u in range(unroll)]
        chains = [(u, jj) for u in range(unroll) for jj in range(npairs)]
        qs = [_stack_heads(q_ref[0, pl.ds(q0s[u], SWA_BLK), jj * LANES_V7X:(jj + 1) * LANES_V7X])
              for u, jj in chains]
        s_c_all = _dot_nt(jnp.concatenate(qs, axis=0), kz)
        p_c, o_w, denom = [], [], []
        for c, (u, jj) in enumerate(chains):
            start = pl.multiple_of(jnp.clip(q0s[u] - SWA_BLK, 0, seq - nwin), SWA_BLK)
            var = (q0s[u] - start) // SWA_BLK
            mb = mask_ref[var]
            s_w = _dot_nt(qs[c], k_ref[0, pl.ds(start, nwin), :]) + jnp.concatenate([mb, mb], axis=0)
            probs, den = _joint_softmax([s_w, s_c_all[c * nq:(c + 1) * nq]], extra_logit=sinks[jj])
            o_w.append(_dot(probs[0], v_ref[0, pl.ds(start, nwin), :]))
            p_c.append(probs[1])
            denom.append(den)
        o_c_all = _dot(jnp.concatenate(p_c, axis=0), vz)
        for c, (u, jj) in enumerate(chains):
            o = (o_w[c] + o_c_all[c * nq:(c + 1) * nq]) / denom[c]
            o_ref[0, pl.ds(q0s[u], SWA_BLK), jj * LANES_V7X:(jj + 1) * LANES_V7X] = (
                _unstack_heads(o).astype(BF16))
        return carry

    lax.fori_loop(0, seq // (SWA_BLK * unroll), body, 0)


def _swa(qkc_x, vc_x, qkc_z, vc_z, mask_bias, sink):
    b, s, _ = qkc_x.shape
    lc = qkc_z.shape[1]
    assert s % (SWA_BLK * SWA_BLKS_PER_ITER) == 0 and s >= 3 * SWA_BLK
    qw = 2 * LANES_V7X
    return pl.pallas_call(
        functools.partial(_swa_kernel, seq=s),
        grid=(SWA_KV_HEADS, b),
        in_specs=[
            pl.BlockSpec(memory_space=pltpu.SMEM),
            pl.BlockSpec((1, s, qw), lambda g, bi: (bi, 0, g)),
            pl.BlockSpec((1, s, LANES_V7X), lambda g, bi: (bi, 0, 4 + g)),
            pl.BlockSpec((1, s, LANES_V7X), lambda g, bi: (bi, 0, g)),
            pl.BlockSpec((1, lc, LANES_V7X), lambda g, bi: (bi, 0, 4 + g)),
            pl.BlockSpec((1, lc, LANES_V7X), lambda g, bi: (bi, 0, g)),
            pl.BlockSpec((3, SWA_BLK, 3 * SWA_BLK), lambda g, bi: (0, 0, 0)),
        ],
        out_specs=pl.BlockSpec((1, s, qw), lambda g, bi: (bi, 0, g)),
        out_shape=jax.ShapeDtypeStruct((b, s, SWA_HEADS * HEAD_DIM), BF16),
        name="swa",
        compiler_params=_cparams(("parallel", "parallel")),
    )(sink, qkc_x, qkc_x, vc_x, qkc_z, vc_z, mask_bias)


def _swa_mask_bias():
    i = np.arange(SWA_BLK)[:, None]
    j = np.arange(3 * SWA_BLK)[None, :]
    out = np.zeros((3, SWA_BLK, 3 * SWA_BLK), np.float32)
    for v in range(3):
        rel = j - v * SWA_BLK - i
        out[v] = np.where(np.abs(rel) <= SWA_WINDOW, 0.0, NEG_INF)
    return jnp.asarray(out)


def _ctx_attn_kernel(*refs, ngroups, use_sink):
    if use_sink:
        sink_ref, q_ref, k_ref, v_ref, o_ref = refs
    else:
        q_ref, k_ref, v_ref, o_ref = refs
    g = pl.program_id(0)
    k = k_ref[0]
    v = v_ref[0]
    n = q_ref.shape[1]
    row = lax.broadcasted_iota(I32, (2 * n, 1), 0)
    for jj in range(ngroups):
        qs = _stack_heads(q_ref[0, :, jj * LANES_V7X:(jj + 1) * LANES_V7X])
        s = _dot_nt(qs, k)
        sink = None
        if use_sink:
            base = 2 * ngroups * g + 2 * jj
            sink = jnp.where(row < n, sink_ref[base], sink_ref[base + 1]) * LOG2E
        o = _softmax_pv([s], [v], extra_logit=sink)
        o_ref[0, :, jj * LANES_V7X:(jj + 1) * LANES_V7X] = _unstack_heads(o).astype(BF16)


def _ctx_attn_a(qkv_z):
    b, lc, _ = qkv_z.shape
    npair = NA_HEADS // 2
    blk = lambda off: pl.BlockSpec((1, lc, LANES_V7X), lambda hp, bi, off=off: (bi, 0, off + hp))
    return pl.pallas_call(
        functools.partial(_ctx_attn_kernel, ngroups=1, use_sink=False),
        grid=(npair, b),
        in_specs=[blk(0), blk(npair), blk(2 * npair)],
        out_specs=pl.BlockSpec((1, lc, LANES_V7X), lambda hp, bi: (bi, 0, hp)),
        out_shape=jax.ShapeDtypeStruct((b, lc, NA_HEADS * HEAD_DIM), BF16),
        name="ctx_attn_a",
        compiler_params=_cparams(("parallel", "parallel")),
    )(qkv_z, qkv_z, qkv_z)


def _ctx_attn_c(qkc_z, vc_z, sink):
    b, lc, _ = qkc_z.shape
    qw = 2 * LANES_V7X
    return pl.pallas_call(
        functools.partial(_ctx_attn_kernel, ngroups=2, use_sink=True),
        grid=(SWA_KV_HEADS, b),
        in_specs=[
            pl.BlockSpec(memory_space=pltpu.SMEM),
            pl.BlockSpec((1, lc, qw), lambda g, bi: (bi, 0, g)),
            pl.BlockSpec((1, lc, LANES_V7X), lambda g, bi: (bi, 0, 4 + g)),
            pl.BlockSpec((1, lc, LANES_V7X), lambda g, bi: (bi, 0, g)),
        ],
        out_specs=pl.BlockSpec((1, lc, qw), lambda g, bi: (bi, 0, g)),
        out_shape=jax.ShapeDtypeStruct((b, lc, SWA_HEADS * HEAD_DIM), BF16),
        name="ctx_attn_c",
        compiler_params=_cparams(("parallel", "parallel")),
    )(sink, qkc_z, qkc_z, vc_z)


PAD_ROWS = 8


def _pool_kernel(u_ref, wp_ref, ps_ref, o_ref, pad_ref, *, length, chunk):
    zeros = jnp.zeros((PAD_ROWS, POOL_GW), F32)
    pad_ref[0:PAD_ROWS, :] = zeros
    pad_ref[length + PAD_ROWS:length + 2 * PAD_ROWS, :] = zeros
    for g, w in enumerate(POOL_WINDOWS):
        half = w // 2
        assert half <= PAD_ROWS
        lanes = slice(g * POOL_GW, (g + 1) * POOL_GW)
        pad_ref[PAD_ROWS:length + PAD_ROWS, :] = u_ref[0, :, lanes]
        for c in range(length // chunk):
            c0 = c * chunk
            acc = pad_ref[c0 + PAD_ROWS - half:c0 + PAD_ROWS - half + chunk, :]
            for j in range(-half + 1, half):
                acc = acc + pad_ref[c0 + PAD_ROWS + j:c0 + PAD_ROWS + j + chunk, :]
            t = c0 + lax.broadcasted_iota(I32, (chunk, 1), 0)
            cnt = (jnp.minimum(t + half, length) - jnp.maximum(t - half, 0)).astype(F32)
            centre = pad_ref[c0 + PAD_ROWS:c0 + PAD_ROWS + chunk, :]
            d = acc / cnt - centre
            y = _dot(d.astype(BF16), wp_ref[g]) * ps_ref[:, lanes]
            o_ref[0, c0:c0 + chunk, lanes] = y.astype(BF16)


def _pool(u, w_pool, pool_scale):
    b, l, width = u.shape
    chunk = min(POOL_CHUNK, l)
    return pl.pallas_call(
        functools.partial(_pool_kernel, length=l, chunk=chunk),
        grid=(b,),
        in_specs=[
            pl.BlockSpec((1, l, width), lambda bi: (bi, 0, 0)),
            pl.BlockSpec(w_pool.shape, lambda bi: (0, 0, 0)),
            pl.BlockSpec((1, width), lambda bi: (0, 0)),
        ],
        out_specs=pl.BlockSpec((1, l, width), lambda bi: (bi, 0, 0)),
        out_shape=jax.ShapeDtypeStruct((b, l, width), BF16),
        scratch_shapes=[pltpu.VMEM((l + 2 * PAD_ROWS, POOL_GW), F32)],
        name="pool",
        compiler_params=_cparams(("parallel",)),
    )(u, w_pool, pool_scale.reshape(1, width))


ROUTER_COLS = LANES_V7X


def _sigmoid(x):
    return 0.5 * jnp.tanh(0.5 * x) + 0.5


def _merge_kernel(ya_ref, yb_ref, yc_ref, gt_ref, x_ref, wb_ref, wo_ref, g1_ref, n2_ref, sh_ref, sc_ref,
                  wr_ref, br_ref, xo_ref, hp_ref, re_ref, rw_ref):
    tm = x_ref.shape[1]
    sub = tm // MERGE_SPLIT
    for part in range(MERGE_SPLIT):
        _merge_rows(slice(part * sub, (part + 1) * sub), ya_ref, yb_ref, yc_ref, gt_ref, x_ref, wb_ref, wo_ref,
                    g1_ref, n2_ref, sh_ref, sc_ref, wr_ref, br_ref, xo_ref, hp_ref, re_ref, rw_ref)


def _merge_rows(rows, ya_ref, yb_ref, yc_ref, gt_ref, x_ref, wb_ref, wo_ref, g1_ref, n2_ref, sh_ref, sc_ref,
                wr_ref, br_ref, xo_ref, hp_ref, re_ref, rw_ref):
    d = D_MODEL
    m = None
    for j, y_ref in enumerate((ya_ref, yb_ref, yc_ref)):
        gate = _sigmoid(gt_ref[0, rows, j * d:(j + 1) * d].astype(F32))
        term = gate * _dot(y_ref[0, rows, :], wb_ref[j])
        m = term if m is None else m + term
    out = _dot(m.astype(BF16), wo_ref[...])
    xn = x_ref[0, rows, :] + g1_ref[0] * out
    xo_ref[0, rows, :] = xn

    ms = jnp.mean(xn * xn, axis=-1, keepdims=True)
    h = xn * lax.rsqrt(ms + NORM_EPS) * n2_ref[...]
    h = h * (1.0 + sc_ref[0]) + sh_ref[0]

    half = d // 2
    h_hi = h.astype(BF16)
    h_hi32 = h_hi.astype(F32)
    lo = pltpu.bitcast(h_hi32[:, :half], U32)
    hi = pltpu.bitcast(h_hi32[:, half:], U32)
    _store_row_tiles(hp_ref, (), rows.start, (lo >> 16) | (hi & jnp.uint32(0xFFFF0000)))

    h_lo = (h - h_hi32).astype(BF16)
    cross = _dot(h_hi, wr_ref[...])
    logits = (cross[:, :ROUTER_COLS] + cross[:, ROUTER_COLS:] + _dot(h_lo, wr_ref[:, :ROUTER_COLS])
              + br_ref[...])
    logits = logits.T
    tm = logits.shape[1]
    le = logits[0:N_EXPERTS]
    lg = logits[N_EXPERTS:N_EXPERTS + N_GROUPS]
    gi = lax.broadcasted_iota(I32, (N_GROUPS, tm), 0).astype(F32)
    lg_max = lg.max(axis=0, keepdims=True)
    g_top = jnp.where(lg == lg_max, gi, float(N_GROUPS)).min(axis=0, keepdims=True)
    p_grp = 1.0 / jnp.exp(lg - lg_max).sum(axis=0, keepdims=True)
    ei_int = lax.broadcasted_iota(I32, (N_EXPERTS, tm), 0)
    ei = ei_int.astype(F32)
    eg = lax.shift_right_logical(ei_int, 3).astype(F32)
    lm = jnp.where(eg == g_top, le, NEG_INF)
    m1 = lm.max(axis=0, keepdims=True)
    i1 = jnp.where(lm == m1, ei, float(N_EXPERTS)).min(axis=0, keepdims=True)
    lm2 = jnp.where(ei == i1, NEG_INF, lm)
    m2 = lm2.max(axis=0, keepdims=True)
    i2 = jnp.where(lm2 == m2, ei, float(N_EXPERTS)).min(axis=0, keepdims=True)
    a2 = jnp.exp(m2 - m1)
    w1 = p_grp / (1.0 + a2)
    re_ref[:, rows] = jnp.concatenate([i1, i2], axis=0).astype(I32)
    rw_ref[:, rows] = jnp.concatenate([w1, w1 * a2], axis=0)


def _merge(ya, yb, yc, gates, x, wb, wo, gate1, norm2_g, shift2, scale2, wr, br):
    b, l, d = x.shape
    tm = min(TM_MERGE, l)
    nt = l // tm
    tok = lambda bi, i: (bi, i, 0)
    per_b = lambda bi, i: (bi, 0, 0)
    const2 = lambda bi, i: (0, 0)
    in_specs = [
        pl.BlockSpec((1, tm, 512), tok), pl.BlockSpec((1, tm, 512), tok), pl.BlockSpec((1, tm, 512), tok),
        pl.BlockSpec((1, tm, 3 * d), tok), pl.BlockSpec((1, tm, d), tok),
        pl.BlockSpec((3, 512, d), lambda bi, i: (0, 0, 0)), pl.BlockSpec((d, d), const2),
        pl.BlockSpec((1, 1, d), per_b), pl.BlockSpec((1, d), const2),
        pl.BlockSpec((1, 1, d), per_b), pl.BlockSpec((1, 1, d), per_b),
        pl.BlockSpec((d, 2 * ROUTER_COLS), const2), pl.BlockSpec((1, ROUTER_COLS), const2),
    ]
    args = [ya, yb, yc, gates, x, wb, wo, gate1, norm2_g.reshape(1, d), shift2, scale2, wr, br]
    return pl.pallas_call(
        _merge_kernel,
        grid=(b, nt),
        in_specs=in_specs,
        out_specs=[
            pl.BlockSpec((1, tm, d), tok),
            pl.BlockSpec((tm * XROWS, LANES_V7X), lambda bi, i: (bi * nt + i, 0)),
            pl.BlockSpec((2, tm), lambda bi, i: (0, bi * nt + i)),
            pl.BlockSpec((2, tm), lambda bi, i: (0, bi * nt + i)),
        ],
        out_shape=[
            jax.ShapeDtypeStruct((b, l, d), F32),
            jax.ShapeDtypeStruct((b * l * XROWS, LANES_V7X), U32),
            jax.ShapeDtypeStruct((2, b * l), I32),
            jax.ShapeDtypeStruct((2, b * l), F32),
        ],
        name="merge_router",
        compiler_params=_cparams(("arbitrary", "arbitrary")),
    )(*args)


def _plan_kernel(re_ref, tri_ref, dest_ref, be_ref, na_ref, cnt_ref, *, ntok, nblk_pad):
    nt = ntok // PLAN_TILE
    ei = lax.broadcasted_iota(I32, (N_EXPERTS, PLAN_TILE), 0)
    cnt_ref[...] = jnp.zeros_like(cnt_ref)

    def rank_tile(i, carry):
        c0 = pl.multiple_of(i * PLAN_TILE, PLAN_TILE)
        e = re_ref[:, pl.ds(c0, PLAN_TILE)]
        hit1 = ei == e[0:1]
        hit2 = ei == e[1:2]
        onehot = jnp.where(hit1 | hit2, 1.0, 0.0)
        before = _dot(onehot.astype(BF16), tri_ref[...]) + cnt_ref[:, 0:1]
        r1 = jnp.where(hit1, before, 0.0).sum(axis=0, keepdims=True)
        r2 = jnp.where(hit2, before, 0.0).sum(axis=0, keepdims=True)
        dest_ref[:, pl.ds(c0, PLAN_TILE)] = jnp.concatenate([r1, r2], axis=0).astype(I32)
        cnt_ref[...] = cnt_ref[...] + onehot.sum(axis=1, keepdims=True)
        return carry

    lax.fori_loop(0, nt, rank_tile, 0)

    cnt = cnt_ref[...]
    padded = jnp.floor((cnt + (MOE_BLK - 1)) * (1.0 / MOE_BLK)) * MOE_BLK
    sub = lax.broadcasted_iota(I32, (N_EXPERTS, LANES_V7X), 0)
    pstart = jnp.zeros_like(padded)
    for e in range(N_EXPERTS - 1):
        pstart = pstart + jnp.where(sub > e, padded[e:e + 1, :], 0.0)
    pend = pstart + padded
    cnt_ref[...] = pstart

    def dest_tile(i, carry):
        c0 = pl.multiple_of(i * PLAN_TILE, PLAN_TILE)
        e = re_ref[:, pl.ds(c0, PLAN_TILE)]
        ps = cnt_ref[:, 0:1]
        s1 = jnp.where(ei == e[0:1], ps, 0.0).sum(axis=0, keepdims=True)
        s2 = jnp.where(ei == e[1:2], ps, 0.0).sum(axis=0, keepdims=True)
        dest_ref[:, pl.ds(c0, PLAN_TILE)] = (dest_ref[:, pl.ds(c0, PLAN_TILE)]
                                              + jnp.concatenate([s1, s2], axis=0).astype(I32))
        return carry

    lax.fori_loop(0, nt, dest_tile, 0)

    first_row = (lax.broadcasted_iota(I32, (N_EXPERTS, nblk_pad), 1) * MOE_BLK).astype(F32)
    owner = jnp.where(pend[:, 0:1] <= first_row, 1.0, 0.0).sum(axis=0, keepdims=True)
    be_ref[...] = jnp.minimum(owner, float(N_EXPERTS - 1)).astype(I32)
    na_ref[...] = (pend[N_EXPERTS - 1:N_EXPERTS, :] * (1.0 / MOE_BLK)).astype(I32)


def _moe_slots(ntok):
    nblk = -(-(2 * ntok + N_EXPERTS * (MOE_BLK - 1)) // MOE_BLK)
    return nblk, -(-nblk // LANES_V7X) * LANES_V7X


def _plan(re, tri):
    ntok = re.shape[1]
    assert ntok % PLAN_TILE == 0
    _, nblk_pad = _moe_slots(ntok)
    return pl.pallas_call(
        functools.partial(_plan_kernel, ntok=ntok, nblk_pad=nblk_pad),
        out_shape=[
            jax.ShapeDtypeStruct((2, ntok), I32),
            jax.ShapeDtypeStruct((1, nblk_pad), I32),
            jax.ShapeDtypeStruct((1, LANES_V7X), I32),
        ],
        scratch_shapes=[pltpu.VMEM((N_EXPERTS, LANES_V7X), F32)],
        name="moe_plan",
        compiler_params=pltpu.CompilerParams(vmem_limit_bytes=VMEM_LIMIT_V7X),
    )(re, tri)


def _dispatch_kernel(dest_ref, hp_ref, xs_in_ref, xs_ref, sem):
    del xs_in_ref
    tm = hp_ref.shape[0] // XROWS

    def row_copy(i, k):
        src = hp_ref.at[pl.ds(pl.multiple_of(i * XROWS, XROWS), XROWS)]
        dst = xs_ref.at[pl.ds(pl.multiple_of(dest_ref[k, i] * XROWS, XROWS), XROWS)]
        return pltpu.make_async_copy(src, dst, sem)

    def issue(i, carry):
        for k in range(2):
            row_copy(i, k).start(priority=k)
        return carry

    lax.fori_loop(0, tm, issue, 0, unroll=8)

    def drain(i, carry):
        for k in range(2):
            row_copy(i, k).wait()
        return carry

    lax.fori_loop(0, tm, drain, 0, unroll=8)


def _dispatch(dest, hp, xs, tok_off):
    ntok, width = hp.shape[0] // XROWS, hp.shape[1]
    tm = min(TM_DISPATCH, ntok)
    assert tok_off % tm == 0 and ntok % tm == 0
    off = tok_off // tm
    return pl.pallas_call(
        _dispatch_kernel,
        grid=(ntok // tm,),
        in_specs=[
            pl.BlockSpec((2, tm), lambda i: (0, off + i), memory_space=pltpu.SMEM),
            pl.BlockSpec((tm * XROWS, width), lambda i: (i, 0)),
            pl.BlockSpec(memory_space=pl.ANY),
        ],
        out_specs=pl.BlockSpec(memory_space=pl.ANY),
        out_shape=jax.ShapeDtypeStruct(xs.shape, U32),
        scratch_shapes=[pltpu.SemaphoreType.DMA(())],
        input_output_aliases={2: 0},
        name="moe_dispatch",
        compiler_params=_cparams(("arbitrary",)),
    )(dest, hp, xs)


def _experts_kernel(be_ref, na_ref, xs_ref, wg_ref, wu_ref, wd_ref, y_ref, wgb_ref, wub_ref, wdb_ref):
    i = pl.program_id(0)
    active = i < na_ref[0]
    new_expert = (i == 0) | (be_ref[i] != be_ref[jnp.maximum(i - 1, 0)])

    @pl.when(active & new_expert)
    def _():
        wgb_ref[...] = wg_ref[0].astype(BF16)
        wub_ref[...] = wu_ref[0].astype(BF16)
        wdb_ref[...] = wd_ref[0].astype(BF16)

    @pl.when(active)
    def _():
        half = D_MODEL // 2
        sub = MOE_BLK // EXPERT_SPLIT
        for part in range(EXPERT_SPLIT):
            xw = _load_row_tiles(xs_ref, (), part * sub, sub, XROWS)
            lo = pltpu.bitcast(xw << 16, F32).astype(BF16)
            hi = pltpu.bitcast(xw & jnp.uint32(0xFFFF0000), F32).astype(BF16)
            g = _dot(lo, wgb_ref[0:half, :]) + _dot(hi, wgb_ref[half:, :])
            u = _dot(lo, wub_ref[0:half, :]) + _dot(hi, wub_ref[half:, :])
            a = (g * _sigmoid(g) * u).astype(BF16)
            _store_row_tiles(y_ref, (), part * sub, _dot(a, wdb_ref[...]))

    @pl.when(pl.program_id(0) >= na_ref[0])
    def _():
        y_ref[...] = jnp.zeros_like(y_ref)


def _experts(be, na, xs, wg, wu, wd, layer):
    nslots, width = xs.shape[0] // XROWS, xs.shape[1]
    nblk = nslots // MOE_BLK
    d = D_MODEL
    last = lambda i, na: jnp.minimum(i, na[0] - 1)
    grid_spec = pltpu.PrefetchScalarGridSpec(
        num_scalar_prefetch=2,
        grid=(nblk,),
        in_specs=[
            pl.BlockSpec((MOE_BLK * XROWS, width), lambda i, be, na: (last(i, na), 0)),
            pl.BlockSpec((None, 1, d, D_EXPERT), lambda i, be, na: (layer, be[last(i, na)], 0, 0)),
            pl.BlockSpec((None, 1, d, D_EXPERT), lambda i, be, na: (layer, be[last(i, na)], 0, 0)),
            pl.BlockSpec((None, 1, D_EXPERT, d), lambda i, be, na: (layer, be[last(i, na)], 0, 0)),
        ],
        out_specs=pl.BlockSpec((MOE_BLK * YROWS, LANES_V7X), lambda i, be, na: (i, 0)),
        scratch_shapes=[pltpu.VMEM((d, D_EXPERT), BF16), pltpu.VMEM((d, D_EXPERT), BF16),
                        pltpu.VMEM((D_EXPERT, d), BF16)],
    )
    return pl.pallas_call(
        _experts_kernel,
        grid_spec=grid_spec,
        out_shape=jax.ShapeDtypeStruct((nslots * YROWS, LANES_V7X), F32),
        name="moe_experts",
        compiler_params=_cparams(("arbitrary",)),
    )(be, na, xs, wg, wu, wd)


def _combine_kernel(*refs, final, nsteps):
    if final:
        dest_ref, next_ref, rw_ref, y_ref, x_ref, g_ref, fg_ref, o_ref, ybuf, sems = refs
    else:
        dest_ref, next_ref, rw_ref, y_ref, x_ref, g_ref, o_ref, ybuf, sems = refs
    tm = x_ref.shape[0]
    step = pl.program_id(0)
    slot = lax.rem(step, 2)

    def row_copy(idx_ref, s, t, k):
        src = y_ref.at[pl.ds(pl.multiple_of(idx_ref[k, t] * YROWS, YROWS), YROWS)]
        dst = ybuf.at[s, k, pl.ds(pl.multiple_of(t * YROWS, YROWS), YROWS)]
        return pltpu.make_async_copy(src, dst, sems.at[s])

    def start_all(idx_ref, s):
        def body(t, carry):
            for k in range(2):
                row_copy(idx_ref, s, t, k).start(priority=k)
            return carry

        lax.fori_loop(0, tm, body, 0, unroll=8)

    @pl.when(step == 0)
    def _():
        start_all(dest_ref, 0)

    @pl.when(step + 1 < nsteps)
    def _():
        start_all(next_ref, 1 - slot)

    def drain(t, carry):
        for k in range(2):
            row_copy(dest_ref, slot, t, k).wait()
        return carry

    lax.fori_loop(0, tm, drain, 0, unroll=8)

    w = rw_ref[...]
    eye = lax.broadcasted_iota(I32, (tm, tm), 0) == lax.broadcasted_iota(I32, (tm, tm), 1)
    w0 = jnp.where(eye, w[0:1, :], 0.0).sum(axis=1, keepdims=True)
    w1 = jnp.where(eye, w[1:2, :], 0.0).sum(axis=1, keepdims=True)
    y0 = _load_row_tiles(ybuf, (slot, 0), 0, tm, YROWS)
    y1 = _load_row_tiles(ybuf, (slot, 1), 0, tm, YROWS)
    xn = x_ref[...] + g_ref[0] * (w0 * y0 + w1 * y1)
    if final:
        ms = jnp.mean(xn * xn, axis=-1, keepdims=True)
        xn = xn * lax.rsqrt(ms + NORM_EPS) * fg_ref[...]
    o_ref[...] = xn


def _combine(dest, rw, y, x, gate, tok_off, final_g):
    b, l, d = x.shape
    tm = min(TM_COMBINE, l)
    assert tok_off % tm == 0 and l % tm == 0
    off = tok_off // tm
    per_b = l // tm
    nsteps = b * l // tm
    final = final_g is not None
    in_specs = [
        pl.BlockSpec((2, tm), lambda i: (0, off + i), memory_space=pltpu.SMEM),
        pl.BlockSpec((2, tm), lambda i: (0, off + jnp.minimum(i + 1, nsteps - 1)), memory_space=pltpu.SMEM),
        pl.BlockSpec((2, tm), lambda i: (0, off + i)),
        pl.BlockSpec(memory_space=pl.ANY),
        pl.BlockSpec((tm, d), lambda i: (i, 0)),
        pl.BlockSpec((1, 1, d), lambda i: (i // per_b, 0, 0)),
    ]
    args = [dest, dest, rw, y, x.reshape(b * l, d), gate]
    if final:
        in_specs.append(pl.BlockSpec((1, d), lambda i: (0, 0)))
        args.append(final_g.reshape(1, d))
    out = pl.pallas_call(
        functools.partial(_combine_kernel, final=final, nsteps=nsteps),
        grid=(nsteps,),
        in_specs=in_specs,
        out_specs=pl.BlockSpec((tm, d), lambda i: (i, 0)),
        out_shape=jax.ShapeDtypeStruct((b * l, d), F32),
        scratch_shapes=[pltpu.VMEM((2, 2, tm * YROWS, LANES_V7X), F32), pltpu.SemaphoreType.DMA((2,))],
        name="moe_combine_final" if final else "moe_combine",
        compiler_params=_cparams(("arbitrary",)),
    )(*args)
    return out.reshape(b, l, d)


def _permute_w_in(w):
    ck0, cv0, gt0 = 2560, 2688, 2816
    dup = lambda c0: [w[:, c0 + g * HEAD_DIM:c0 + (g + 1) * HEAD_DIM] for g in range(SWA_KV_HEADS) for _ in range(2)]
    cols = [w[:, :ck0]] + dup(ck0) + dup(cv0) + [w[:, gt0:]]
    return jnp.concatenate(cols, axis=1).astype(BF16)


def _rope_tables(seq):
    t = jnp.arange(seq, dtype=I32)
    row = (t // GRID_W).astype(F32)
    col = (t % GRID_W).astype(F32)
    inv = ROPE_THETA ** (-jnp.arange(0, ROPE_AXIS_DIM, 2, dtype=F32) / ROPE_AXIS_DIM)
    ang_r = row[:, None] * inv[None, :]
    ang_c = col[:, None] * inv[None, :]
    cos = jnp.concatenate([jnp.cos(ang_r)] * 2 + [jnp.cos(ang_c)] * 2, axis=1)
    sin = jnp.concatenate([-jnp.sin(ang_r), jnp.sin(ang_r), -jnp.sin(ang_c), jnp.sin(ang_c)], axis=1)
    return jnp.tile(cos, (1, 2)), jnp.tile(sin, (1, 2))


def _router_params(w_rg, b_rg, w_re, b_re):
    d = w_rg.shape[0]
    pad = ROUTER_COLS - N_EXPERTS - N_GROUPS
    w = jnp.concatenate([w_re, w_rg, jnp.zeros((d, pad), F32)], axis=1)
    w_hi = w.astype(BF16)
    w_lo = (w - w_hi.astype(F32)).astype(BF16)
    br = jnp.concatenate([b_re, b_rg, jnp.zeros((pad,), F32)]).reshape(1, ROUTER_COLS)
    return jnp.concatenate([w_hi, w_lo], axis=1), br


def _moe(re, hp_parts, wg, wu, wd, layer, tri):
    ntok = re.shape[1]
    nblk, _ = _moe_slots(ntok)
    dest, be, na = _plan(re, tri)
    xs = jnp.zeros((nblk * MOE_BLK * XROWS, LANES_V7X), U32)
    tok_off = 0
    for hp in hp_parts:
        xs = _dispatch(dest, hp, xs, tok_off)
        tok_off += hp.shape[0] // XROWS
    y = _experts(be.reshape(-1), na.reshape(-1)[:1], xs, wg, wu, wd, layer)
    return dest, y


def kernel(x, c, ctx, c_ctx, w_mod, b_mod, norm1_g, norm2_g, w_in, rpb_a, w_pool, pool_scale, sink_c,
           w_branch, w_out, w_router_group, b_router_group, w_router_expert, b_router_expert,
           w_exp_gate, w_exp_up, w_exp_down, final_g):
    b, s, d = x.shape
    lc = ctx.shape[1]
    depth = w_mod.shape[0]
    nz = b * lc

    pad_rows = (-(b + 1)) % 8
    cc = jnp.concatenate([c, c_ctx[None, :], jnp.zeros((pad_rows, d), F32)], axis=0)
    mod = _modvec(cc, w_mod, b_mod)
    cos, sin = _rope_tables(s)
    mask_bias = _swa_mask_bias()
    tri = jnp.asarray(np.triu(np.ones((PLAN_TILE, PLAN_TILE), np.float32), k=1), BF16)

    z = ctx
    for l in range(depth):
        last = l == depth - 1
        mx = [mod[l, :b, j * d:(j + 1) * d].reshape(b, 1, d) for j in range(N_MOD)]
        mz = [jnp.broadcast_to(mod[l, b:b + 1, j * d:(j + 1) * d], (b, d)).reshape(b, 1, d) for j in range(N_MOD)]
        w_perm = _permute_w_in(w_in[l])
        wp = w_pool[l].astype(BF16)
        wb = w_branch[l].astype(BF16)
        wo = w_out[l].astype(BF16)
        wr, br = _router_params(w_router_group[l], b_router_group[l], w_router_expert[l], b_router_expert[l])

        qkv_x, u_x, qkc_x, vc_x, gt_x = _inproj(x, mx[0], mx[1], norm1_g[l], w_perm, cos, sin)
        qkv_z, u_z, qkc_z, vc_z, gt_z = _inproj(z, mz[0], mz[1], norm1_g[l], w_perm, None, None)
        ya = _natten(qkv_x, qkv_z, _natten_bias(rpb_a[l]))
        yb = _pool(u_x, wp, pool_scale[l])
        yc = _swa(qkc_x, vc_x, qkc_z, vc_z, mask_bias, sink_c[l])

        x, hp_x, re, rw = _merge(ya, yb, yc, gt_x, x, wb, wo, mx[2], norm2_g[l], mx[3], mx[4], wr, br)
        hp_parts = [hp_x]
        x_off = 0
        if not last:
            za = _ctx_attn_a(qkv_z)
            zb = _pool(u_z, wp, pool_scale[l])
            zc = _ctx_attn_c(qkc_z, vc_z, sink_c[l])
            z, hp_z, re_z, rw_z = _merge(za, zb, zc, gt_z, z, wb, wo, mz[2], norm2_g[l], mz[3], mz[4], wr, br)
            re = jnp.concatenate([re_z, re], axis=1)
            rw = jnp.concatenate([rw_z, rw], axis=1)
            hp_parts = [hp_z, hp_x]
            x_off = nz

        dest, y = _moe(re, hp_parts, w_exp_gate, w_exp_up, w_exp_down, l, tri)
        if not last:
            z = _combine(dest, rw, y, z, mz[5], 0, None)
        x = _combine(dest, rw, y, x, mx[5], x_off, final_g if last else None)
    return x
```
